```python
import math
import jax, jax.numpy as jnp
from jax import lax
import numpy as np

D_MODEL = 1024
BATCH = 4
SEQ = 4096
DEPTH = 1

PLE_DIM = 256
N_HEADS = 4
HEAD_DIM = 64
ATTN_W = N_HEADS * 2 * HEAD_DIM
N_POOL_GROUPS = 4
POOL_WINDOWS = (2, 4, 8, 16)
POOL_CH = 128
POOL_W = N_POOL_GROUPS * POOL_CH
MIX_W = ATTN_W + POOL_W
IN_W = 4 * ATTN_W + 2 * POOL_W
QBLK = 128
ALPHA = (2.0 * DEPTH) ** 0.25
BETA = (8.0 * DEPTH) ** -0.25
LN_EPS = 1e-5

kernel_name = "hybrid_diffattn_multipool_deepnorm"


def lambda_init(layer_idx):
    return 0.8 - 0.6 * math.exp(-0.3 * layer_idx)


def layer_norm(x, g, b):
    xf = x.astype(jnp.float32)
    mu = jnp.mean(xf, axis=-1, keepdims=True)
    var = jnp.mean(jnp.square(xf - mu), axis=-1, keepdims=True)
    y = (xf - mu) * lax.rsqrt(var + LN_EPS)
    return (y * g.astype(jnp.float32) + b.astype(jnp.float32)).astype(x.dtype)


def rms_norm(x, g):
    xf = x.astype(jnp.float32)
    y = xf * lax.rsqrt(jnp.mean(jnp.square(xf), axis=-1, keepdims=True) + LN_EPS)
    return (y * g.astype(jnp.float32)).astype(x.dtype)


def diff_attention(q, k, v, lam):
    B, S, H, _, Dh = q.shape
    nb = S // QBLK
    scale = 1.0 / math.sqrt(Dh)
    qb = q.reshape(B, nb, QBLK, H, 2, Dh).transpose(1, 0, 2, 3, 4, 5)
    kpos = jnp.arange(S)

    def block(args):
        qi, bi = args
        s = jnp.einsum('bqhcd,bkhcd->bhcqk', qi, k).astype(jnp.float32) * scale
        qpos = bi * QBLK + jnp.arange(QBLK)
        causal = kpos[None, :] <= qpos[:, None]
        s = jnp.where(causal, s, -jnp.inf)
        a = jax.nn.softmax(s, axis=-1)
        a = a[:, :, 0] - lam * a[:, :, 1]
        return jnp.einsum('bhqk,bkhe->bqhe', a.astype(v.dtype), v)

    o = lax.map(block, (qb, jnp.arange(nb)))
    return o.transpose(1, 0, 2, 3, 4).reshape(B, S, H, 2 * Dh)


def multiscale_pool(u, w_pool, scale):
    B, S, _ = u.shape
    ug = u.reshape(B, S, N_POOL_GROUPS, POOL_CH)
    c = jnp.cumsum(ug.astype(jnp.float32), axis=1)
    t = jnp.arange(1, S + 1, dtype=jnp.float32)
    means = []
    for g, w in enumerate(POOL_WINDOWS):
        cg = c[:, :, g]
        lag = jnp.pad(cg, ((0, 0), (w, 0), (0, 0)))[:, :S]
        cnt = jnp.minimum(t, float(w))[None, :, None]
        means.append((cg - lag) / cnt)
    mean = jnp.stack(means, axis=2).astype(u.dtype)
    d = mean - ug
    y = jnp.einsum('bsgc,gcd->bsgd', d, w_pool)
    return y.reshape(B, S, POOL_W) * scale


def setup_inputs(seed: int = 0) -> dict:
    key = jax.random.key(seed)
    ks = jax.random.split(key, 16)
    f32 = jnp.float32
    nrm = lambda k, shape, s: (jax.random.normal(k, shape, f32) * s)
    return {
        "x": nrm(ks[0], (BATCH, SEQ, D_MODEL), 1.0),
        "p": nrm(ks[1], (DEPTH, BATCH, SEQ, PLE_DIM), 1.0),
        "w_in": nrm(ks[2], (DEPTH, D_MODEL, IN_W), D_MODEL ** -0.5),
        "lam_q1": nrm(ks[3], (DEPTH, HEAD_DIM), 0.1),
        "lam_k1": nrm(ks[4], (DEPTH, HEAD_DIM), 0.1),
        "lam_q2": nrm(ks[5], (DEPTH, HEAD_DIM), 0.1),
        "lam_k2": nrm(ks[6], (DEPTH, HEAD_DIM), 0.1),
        "subln_g": 1.0 + nrm(ks[7], (DEPTH, 2 * HEAD_DIM), 0.05),
        "w_pool": nrm(ks[8], (DEPTH, N_POOL_GROUPS, POOL_CH, POOL_CH), POOL_CH ** -0.5),
        "pool_scale": 1.0 + nrm(ks[9], (DEPTH, POOL_W), 0.1),
        "w_out": nrm(ks[10], (DEPTH, MIX_W, D_MODEL), MIX_W ** -0.5 * BETA),
        "ln_g": 1.0 + nrm(ks[11], (DEPTH, D_MODEL), 0.05),
        "ln_b": nrm(ks[12], (DEPTH, D_MODEL), 0.02),
        "w_pe": nrm(ks[13], (DEPTH, PLE_DIM, D_MODEL), PLE_DIM ** -0.5),
        "w_pg": nrm(ks[14], (DEPTH, D_MODEL, D_MODEL), D_MODEL ** -0.5),
        "b_pg": nrm(ks[15], (DEPTH, D_MODEL), 0.02),
    }


def reference(x, p, w_in, lam_q1, lam_k1, lam_q2, lam_k2, subln_g, w_pool, pool_scale,
              w_out, ln_g, ln_b, w_pe, w_pg, b_pg):
    B, S, _ = x.shape
    h = x
    for i in range(DEPTH):
        lam_init = lambda_init(i)
        z = h @ w_in[i]
        q = z[..., 0:ATTN_W].reshape(B, S, N_HEADS, 2, HEAD_DIM)
        k = z[..., ATTN_W:2 * ATTN_W].reshape(B, S, N_HEADS, 2, HEAD_DIM)
        v = z[..., 2 * ATTN_W:3 * ATTN_W].reshape(B, S, N_HEADS, 2 * HEAD_DIM)
        g_attn = z[..., 3 * ATTN_W:4 * ATTN_W]
        u = z[..., 4 * ATTN_W:4 * ATTN_W + POOL_W]
        g_pool = z[..., 4 * ATTN_W + POOL_W:]

        lam = (jnp.exp(jnp.sum(lam_q1[i].astype(jnp.float32) * lam_k1[i].astype(jnp.float32)))
               - jnp.exp(jnp.sum(lam_q2[i].astype(jnp.float32) * lam_k2[i].astype(jnp.float32)))
               + lam_init)
        o = diff_attention(q, k, v, lam)
        o = rms_norm(o, subln_g[i]) * (1.0 - lam_init)
        attn_out = o.reshape(B, S, ATTN_W) * jax.nn.silu(g_attn)

        pool_out = multiscale_pool(u, w_pool[i], pool_scale[i]) * jax.nn.silu(g_pool)

        mix = jnp.concatenate([attn_out, pool_out], axis=-1) @ w_out[i]
        h = layer_norm(ALPHA * h + mix, ln_g[i], ln_b[i])

        gate = jax.nn.sigmoid(h @ w_pg[i] + b_pg[i])
        h = h + (p[i] @ w_pe[i]) * gate
    return h
```

```python
import functools
import math

import jax
import jax.numpy as jnp
from jax import lax
from jax.experimental import pallas as pl
from jax.experimental.pallas import tpu as pltpu

D_MODEL = 1024
PLE_DIM = 256
N_HEADS = 4
HEAD_DIM = 64
HEAD_W = 2 * HEAD_DIM
ATTN_W = N_HEADS * HEAD_W
N_POOL_GROUPS = 4
POOL_WINDOWS = (2, 4, 8, 16)
POOL_CH = 128
POOL_W = N_POOL_GROUPS * POOL_CH
IN_W = 4 * ATTN_W + 2 * POOL_W
LN_EPS = 1e-5
POOL_HALO = max(POOL_WINDOWS)

V7X_VMEM_LIMIT_BYTES = 56 * 1024 * 1024

BF16 = jnp.bfloat16
F32 = jnp.float32


def _lambda_init(layer_idx):
    return 0.8 - 0.6 * math.exp(-0.3 * layer_idx)


def _silu(g):
    return g / (1.0 + jnp.exp(-g))


def _in_proj_pool_kernel(x_ref, w_ref, wpool_ref, pscale_ref,
                         qkv_ref, gattn_ref, pool_ref, ubuf_ref, *, tm):
    i = pl.program_id(1)
    xb = x_ref[...].astype(BF16)

    def proj(lo, width):
        return jnp.dot(xb, w_ref[:, lo:lo + width], preferred_element_type=F32)

    scale = 1.0 / math.sqrt(HEAD_DIM)
    qkv_ref[:, 0:ATTN_W] = (proj(0, ATTN_W) * scale).astype(BF16)
    qkv_ref[:, ATTN_W:2 * ATTN_W] = proj(ATTN_W, ATTN_W).astype(BF16)
    qkv_ref[:, 2 * ATTN_W:3 * ATTN_W] = proj(2 * ATTN_W, ATTN_W).astype(BF16)
    gattn_ref[...] = proj(3 * ATTN_W, ATTN_W)

    @pl.when(i == 0)
    def _():
        ubuf_ref[0:POOL_HALO, :] = jnp.zeros((POOL_HALO, POOL_W), F32)

    @pl.when(i > 0)
    def _():
        ubuf_ref[0:POOL_HALO, :] = ubuf_ref[tm:tm + POOL_HALO, :]

    ubuf_ref[POOL_HALO:, :] = proj(4 * ATTN_W, POOL_W)
    g_pool = proj(4 * ATTN_W + POOL_W, POOL_W)

    t = (i * tm + lax.broadcasted_iota(jnp.int32, (tm, 1), 0) + 1).astype(F32)
    for g, w in enumerate(POOL_WINDOWS):
        cols = slice(g * POOL_CH, (g + 1) * POOL_CH)
        ext = ubuf_ref[:, cols]
        acc = ext
        span = 1
        while span < w:
            acc = acc + pltpu.roll(acc, span, axis=0)
            span *= 2
        u = ext[POOL_HALO:, :]
        mean = acc[POOL_HALO:, :] / jnp.minimum(t, float(w))
        d = (mean - u).astype(BF16)
        y = jnp.dot(d, wpool_ref[g], preferred_element_type=F32)
        y = y * pscale_ref[:, cols] * _silu(g_pool[:, cols])
        pool_ref[:, cols] = y.astype(BF16)


def _in_proj_pool(x, w_in_bf, w_pool_bf, pool_scale, *, tm):
    B, S, _ = x.shape
    kern = functools.partial(_in_proj_pool_kernel, tm=tm)
    return pl.pallas_call(
        kern,
        grid=(B, S // tm),
        in_specs=[
            pl.BlockSpec((None, tm, D_MODEL), lambda b, i: (b, i, 0)),
            pl.BlockSpec((D_MODEL, IN_W), lambda b, i: (0, 0)),
            pl.BlockSpec((N_POOL_GROUPS, POOL_CH, POOL_CH), lambda b, i: (0, 0, 0)),
            pl.BlockSpec((1, POOL_W), lambda b, i: (0, 0)),
        ],
        out_specs=[
            pl.BlockSpec((None, tm, 3 * ATTN_W), lambda b, i: (b, i, 0)),
            pl.BlockSpec((None, tm, ATTN_W), lambda b, i: (b, i, 0)),
            pl.BlockSpec((None, tm, POOL_W), lambda b, i: (b, i, 0)),
        ],
        out_shape=[
            jax.ShapeDtypeStruct((B, S, 3 * ATTN_W), BF16),
            jax.ShapeDtypeStruct((B, S, ATTN_W), F32),
            jax.ShapeDtypeStruct((B, S, POOL_W), BF16),
        ],
        scratch_shapes=[pltpu.VMEM((POOL_HALO + tm, POOL_W), F32)],
        compiler_params=pltpu.CompilerParams(
            dimension_semantics=("arbitrary", "arbitrary"),
            vmem_limit_bytes=V7X_VMEM_LIMIT_BYTES),
        name="in_proj_pool",
    )(x, w_in_bf, w_pool_bf, pool_scale)


def _diff_attn_kernel(lq1_ref, lk1_ref, lq2_ref, lk2_ref, subln_ref,
                      q_ref, k_ref, v_ref, g_ref, o_ref,
                      m_ref, l_ref, acc_ref, *, tq, tk, lam_init):
    qi = pl.program_id(1)
    heads = range(N_HEADS)

    lane = lax.broadcasted_iota(jnp.int32, (1, HEAD_W), 1)
    first_half = lane < HEAD_DIM

    qs = []
    for h in heads:
        qh = q_ref[:, h * HEAD_W:(h + 1) * HEAD_W]
        zero = jnp.zeros_like(qh)
        qs.append(jnp.concatenate(
            [jnp.where(first_half, qh, zero), jnp.where(first_half, zero, qh)], axis=0))

    m_ref[...] = jnp.full(m_ref.shape, -jnp.inf, F32)
    l_ref[...] = jnp.zeros(l_ref.shape, F32)
    acc_ref[...] = jnp.zeros(acc_ref.shape, F32)

    def step(k_start, width, masked):
        for h in heads:
            cols = slice(h * HEAD_W, (h + 1) * HEAD_W)
            kj = k_ref[pl.ds(k_start, width), cols]
            vj = v_ref[pl.ds(k_start, width), cols]
            s = lax.dot_general(qs[h], kj, (((1,), (1,)), ((), ())),
                                preferred_element_type=F32)
            if masked:
                row = lax.broadcasted_iota(jnp.int32, (2 * tq, width), 0)
                row = jnp.where(row >= tq, row - tq, row)
                col = lax.broadcasted_iota(jnp.int32, (2 * tq, width), 1)
                s = jnp.where(col <= row, s, -jnp.inf)
            m_prev = m_ref[h]
            m_new = jnp.maximum(m_prev, jnp.max(s, axis=-1, keepdims=True))
            alpha = jnp.exp(m_prev - m_new)
            p = jnp.exp(s - m_new)
            l_ref[h] = alpha * l_ref[h] + jnp.sum(p, axis=-1, keepdims=True)
            acc_ref[h] = alpha * acc_ref[h] + jnp.dot(
                p.astype(BF16), vj, preferred_element_type=F32)
            m_ref[h] = m_new

    def body(j, carry):
        step(pl.multiple_of(j * tk, tk), tk, masked=False)
        return carry

    lax.fori_loop(0, qi * (tq // tk), body, 0)
    step(pl.multiple_of(qi * tq, tq), tq, masked=True)

    lam = (jnp.exp(jnp.sum(lq1_ref[...] * lk1_ref[...], axis=-1, keepdims=True))
           - jnp.exp(jnp.sum(lq2_ref[...] * lk2_ref[...], axis=-1, keepdims=True))
           + lam_init)
    for h in heads:
        cols = slice(h * HEAD_W, (h + 1) * HEAD_W)
        o_all = acc_ref[h] / l_ref[h]
        o = o_all[:tq] - lam * o_all[tq:]
        o = o * lax.rsqrt(jnp.mean(o * o, axis=-1, keepdims=True) + LN_EPS)
        o = o * subln_ref[...] * (1.0 - lam_init)
        o_ref[:, cols] = (o * _silu(g_ref[:, cols])).astype(BF16)


def _diff_attn(qkv, g_attn, lam_q1, lam_k1, lam_q2, lam_k2, subln_g, *, tq, tk, lam_init):
    B, S, _ = qkv.shape
    kern = functools.partial(_diff_attn_kernel, tq=tq, tk=tk, lam_init=lam_init)
    vec = lambda n: pl.BlockSpec((1, n), lambda b, i: (0, 0))
    return pl.pallas_call(
        kern,
        grid=(B, S // tq),
        in_specs=[
            vec(HEAD_DIM), vec(HEAD_DIM), vec(HEAD_DIM), vec(HEAD_DIM), vec(HEAD_W),
            pl.BlockSpec((None, tq, ATTN_W), lambda b, i: (b, i, 0)),
            pl.BlockSpec((None, S, ATTN_W), lambda b, i: (b, 0, 1)),
            pl.BlockSpec((None, S, ATTN_W), lambda b, i: (b, 0, 2)),
            pl.BlockSpec((None, tq, ATTN_W), lambda b, i: (b, i, 0)),
        ],
        out_specs=pl.BlockSpec((None, tq, ATTN_W), lambda b, i: (b, i, 0)),
        out_shape=jax.ShapeDtypeStruct((B, S, ATTN_W), BF16),
        scratch_shapes=[
            pltpu.VMEM((N_HEADS, 2 * tq, 1), F32),
            pltpu.VMEM((N_HEADS, 2 * tq, 1), F32),
            pltpu.VMEM((N_HEADS, 2 * tq, HEAD_W), F32),
        ],
        compiler_params=pltpu.CompilerParams(
            dimension_semantics=("arbitrary", "arbitrary"),
            vmem_limit_bytes=V7X_VMEM_LIMIT_BYTES),
        name="diff_attn",
    )(lam_q1, lam_k1, lam_q2, lam_k2, subln_g, qkv, qkv, qkv, g_attn)


def _out_proj_kernel(x_ref, attn_ref, pool_ref, p_ref, wout_ref, lng_ref, lnb_ref,
                     wpe_ref, wpg_ref, bpg_ref, o_ref, *, alpha):
    mix = (jnp.dot(attn_ref[...], wout_ref[0:ATTN_W, :], preferred_element_type=F32)
           + jnp.dot(pool_ref[...], wout_ref[ATTN_W:, :], preferred_element_type=F32))
    r = alpha * x_ref[...] + mix
    mu = jnp.mean(r, axis=-1, keepdims=True)
    c = r - mu
    var = jnp.mean(c * c, axis=-1, keepdims=True)
    h = c * lax.rsqrt(var + LN_EPS) * lng_ref[...] + lnb_ref[...]
    logits = jnp.dot(h.astype(BF16), wpg_ref[...], preferred_element_type=F32) + bpg_ref[...]
    gate = 1.0 / (1.0 + jnp.exp(-logits))
    pe = jnp.dot(p_ref[...].astype(BF16), wpe_ref[...], preferred_element_type=F32)
    o_ref[...] = h + pe * gate


def _out_proj(x2, attn2, pool2, p2, w_out_bf, ln_g, ln_b, w_pe_bf, w_pg_bf, b_pg, *, tm, alpha):
    R = x2.shape[0]
    kern = functools.partial(_out_proj_kernel, alpha=alpha)
    rows = lambda n: pl.BlockSpec((tm, n), lambda i: (i, 0))
    full = lambda a, b: pl.BlockSpec((a, b), lambda i: (0, 0))
    return pl.pallas_call(
        kern,
        grid=(R // tm,),
        in_specs=[
            rows(D_MODEL), rows(ATTN_W), rows(POOL_W), rows(PLE_DIM),
            full(ATTN_W + POOL_W, D_MODEL), full(1, D_MODEL), full(1, D_MODEL),
            full(PLE_DIM, D_MODEL), full(D_MODEL, D_MODEL), full(1, D_MODEL),
        ],
        out_specs=rows(D_MODEL),
        out_shape=jax.ShapeDtypeStruct((R, D_MODEL), F32),
        compiler_params=pltpu.CompilerParams(
            dimension_semantics=("arbitrary",),
            vmem_limit_bytes=V7X_VMEM_LIMIT_BYTES),
        name="out_proj_ln_ple",
    )(x2, attn2, pool2, p2, w_out_bf, ln_g, ln_b, w_pe_bf, w_pg_bf, b_pg)


def kernel(x, p, w_in, lam_q1, lam_k1, lam_q2, lam_k2, subln_g, w_pool, pool_scale,
           w_out, ln_g, ln_b, w_pe, w_pg, b_pg):
    B, S, D = x.shape
    depth = w_in.shape[0]
    h = x
    for i in range(depth):
        alpha = (2.0 * depth) ** 0.25
        qkv, g_attn, pool_mix = _in_proj_pool(
            h, w_in[i].astype(BF16), w_pool[i].astype(BF16), pool_scale[i][None, :], tm=256)
        attn = _diff_attn(
            qkv, g_attn, lam_q1[i][None, :], lam_k1[i][None, :], lam_q2[i][None, :],
            lam_k2[i][None, :], subln_g[i][None, :], tq=256, tk=256, lam_init=_lambda_init(i))
        out = _out_proj(
            h.reshape(B * S, D), attn.reshape(B * S, ATTN_W), pool_mix.reshape(B * S, POOL_W),
            p[i].reshape(B * S, PLE_DIM), w_out[i].astype(BF16), ln_g[i][None, :],
            ln_b[i][None, :], w_pe[i].astype(BF16), w_pg[i].astype(BF16), b_pg[i][None, :],
            tm=512, alpha=alpha)
        h = out.reshape(B, S, D)
    return h
```

```python
import functools
import math

import jax
import jax.numpy as jnp
from jax import lax
from jax.experimental import pallas as pl
from jax.experimental.pallas import tpu as pltpu

D_MODEL = 1024
PLE_DIM = 256
N_HEADS = 4
HEAD_DIM = 64
HEAD_W = 2 * HEAD_DIM
ATTN_W = N_HEADS * HEAD_W
N_POOL_GROUPS = 4
POOL_WINDOWS = (2, 4, 8, 16)
POOL_CH = 128
POOL_W = N_POOL_GROUPS * POOL_CH
REST_W = ATTN_W + ATTN_W + 2 * POOL_W
LN_EPS = 1e-5
POOL_HALO = max(POOL_WINDOWS)

BF16_SUBLANE_TILE = 16
V_ROWS = HEAD_W + BF16_SUBLANE_TILE

SEQ_TILE = 512
V7X_VMEM_LIMIT_BYTES = 56 * 1024 * 1024

BF16 = jnp.bfloat16
F32 = jnp.float32
NT_DIMS = (((1,), (1,)), ((), ()))


def _lambda_init(layer_idx):
    return 0.8 - 0.6 * math.exp(-0.3 * layer_idx)


def _silu(g):
    return g / (1.0 + jnp.exp(-g))


def _in_proj_pool_kernel(x_ref, wqv_ref, wrest_ref, wpool_ref, pscale_ref,
                         qz_ref, k_ref, vt_ref, gattn_ref, pool_ref, ubuf_ref, *, tm):
    i = pl.program_id(1)
    xb = x_ref[...].astype(BF16)

    scale = 1.0 / math.sqrt(HEAD_DIM)
    q_t = lax.dot_general(wqv_ref[0:ATTN_W, :], xb, NT_DIMS, preferred_element_type=F32)
    v_t = lax.dot_general(wqv_ref[ATTN_W:, :], xb, NT_DIMS, preferred_element_type=F32)
    zeros = jnp.zeros((HEAD_DIM, tm), BF16)
    ones_tile = jnp.where(
        lax.broadcasted_iota(jnp.int32, (BF16_SUBLANE_TILE, tm), 0) == 0, 1.0, 0.0).astype(BF16)
    for h in range(N_HEADS):
        f0 = h * HEAD_W
        qz_ref[f0:f0 + HEAD_DIM, 0:tm] = (q_t[f0:f0 + HEAD_DIM] * scale).astype(BF16)
        qz_ref[f0 + HEAD_DIM:f0 + HEAD_W, 0:tm] = zeros
        qz_ref[f0:f0 + HEAD_DIM, tm:] = zeros
        qz_ref[f0 + HEAD_DIM:f0 + HEAD_W, tm:] = (q_t[f0 + HEAD_DIM:f0 + HEAD_W] * scale).astype(BF16)
        r0 = h * V_ROWS
        vt_ref[r0:r0 + HEAD_W, :] = v_t[f0:f0 + HEAD_W].astype(BF16)
        vt_ref[r0 + HEAD_W:r0 + V_ROWS, :] = ones_tile

    def proj(lo, width):
        return jnp.dot(xb, wrest_ref[:, lo:lo + width], preferred_element_type=F32)

    k_ref[...] = proj(0, ATTN_W).astype(BF16)
    gattn_ref[...] = proj(ATTN_W, ATTN_W)

    @pl.when(i == 0)
    def _():
        ubuf_ref[0:POOL_HALO, :] = jnp.zeros((POOL_HALO, POOL_W), F32)

    @pl.when(i > 0)
    def _():
        ubuf_ref[0:POOL_HALO, :] = ubuf_ref[tm:tm + POOL_HALO, :]

    ubuf_ref[POOL_HALO:, :] = proj(2 * ATTN_W, POOL_W)
    g_pool = proj(2 * ATTN_W + POOL_W, POOL_W)

    t = (i * tm + lax.broadcasted_iota(jnp.int32, (tm, 1), 0) + 1).astype(F32)
    for g, w in enumerate(POOL_WINDOWS):
        cols = slice(g * POOL_CH, (g + 1) * POOL_CH)
        ext = ubuf_ref[:, cols]
        acc = ext
        span = 1
        while span < w:
            acc = acc + pltpu.roll(acc, span, axis=0)
            span *= 2
        u = ext[POOL_HALO:, :]
        mean = acc[POOL_HALO:, :] / jnp.minimum(t, float(w))
        d = (mean - u).astype(BF16)
        y = jnp.dot(d, wpool_ref[g], preferred_element_type=F32)
        y = y * pscale_ref[:, cols] * _silu(g_pool[:, cols])
        pool_ref[:, cols] = y.astype(BF16)


def _in_proj_pool(x, wqv_t, w_rest, w_pool_bf, pool_scale):
    B, S, _ = x.shape
    tm = SEQ_TILE
    nt = S // tm
    kern = functools.partial(_in_proj_pool_kernel, tm=tm)
    return pl.pallas_call(
        kern,
        grid=(B, nt),
        in_specs=[
            pl.BlockSpec((None, tm, D_MODEL), lambda b, i: (b, i, 0)),
            pl.BlockSpec((2 * ATTN_W, D_MODEL), lambda b, i: (0, 0)),
            pl.BlockSpec((D_MODEL, REST_W), lambda b, i: (0, 0)),
            pl.BlockSpec((N_POOL_GROUPS, POOL_CH, POOL_CH), lambda b, i: (0, 0, 0)),
            pl.BlockSpec((1, POOL_W), lambda b, i: (0, 0)),
        ],
        out_specs=[
            pl.BlockSpec((None, None, ATTN_W, 2 * tm), lambda b, i: (b, i, 0, 0)),
            pl.BlockSpec((None, tm, ATTN_W), lambda b, i: (b, i, 0)),
            pl.BlockSpec((None, None, N_HEADS * V_ROWS, tm), lambda b, i: (b, i, 0, 0)),
            pl.BlockSpec((None, tm, ATTN_W), lambda b, i: (b, i, 0)),
            pl.BlockSpec((None, tm, POOL_W), lambda b, i: (b, i, 0)),
        ],
        out_shape=[
            jax.ShapeDtypeStruct((B, nt, ATTN_W, 2 * tm), BF16),
            jax.ShapeDtypeStruct((B, S, ATTN_W), BF16),
            jax.ShapeDtypeStruct((B, nt, N_HEADS * V_ROWS, tm), BF16),
            jax.ShapeDtypeStruct((B, S, ATTN_W), F32),
            jax.ShapeDtypeStruct((B, S, POOL_W), BF16),
        ],
        scratch_shapes=[pltpu.VMEM((POOL_HALO + tm, POOL_W), F32)],
        compiler_params=pltpu.CompilerParams(
            dimension_semantics=("arbitrary", "arbitrary"),
            vmem_limit_bytes=V7X_VMEM_LIMIT_BYTES),
        name="in_proj_pool",
    )(x, wqv_t, w_rest, w_pool_bf, pool_scale)


def _diff_attn_kernel(lq1_ref, lk1_ref, lq2_ref, lk2_ref, subln_ref,
                      qz_ref, k_ref, vt_ref, g_ref, o_ref,
                      m_ref, acc_ref, *, tq, lam_init):
    qi = pl.program_id(1)
    tk = tq
    heads = range(N_HEADS)

    m_ref[...] = jnp.full(m_ref.shape, -jnp.inf, F32)
    acc_ref[...] = jnp.zeros(acc_ref.shape, F32)

    def step(j, masked):
        k_start = pl.multiple_of(j * tk, tk)
        for h in heads:
            kj = k_ref[pl.ds(k_start, tk), h * HEAD_W:(h + 1) * HEAD_W]
            vj = vt_ref[j, h * V_ROWS:(h + 1) * V_ROWS, :]
            s_t = jnp.dot(kj, qz_ref[h * HEAD_W:(h + 1) * HEAD_W, :],
                          preferred_element_type=F32)
            if masked:
                key = lax.broadcasted_iota(jnp.int32, (tk, 2 * tq), 0)
                qry = lax.broadcasted_iota(jnp.int32, (tk, 2 * tq), 1)
                qry = jnp.where(qry >= tq, qry - tq, qry)
                s_t = jnp.where(key <= qry, s_t, -jnp.inf)
            m_prev = m_ref[h]
            m_new = jnp.maximum(m_prev, jnp.max(s_t, axis=0, keepdims=True))
            alpha = jnp.exp(m_prev - m_new)
            p_t = jnp.exp(s_t - m_new).astype(BF16)
            acc_ref[h] = alpha * acc_ref[h] + jnp.dot(vj, p_t, preferred_element_type=F32)
            m_ref[h] = m_new

    def body(j, carry):
        step(j, masked=False)
        return carry

    lax.fori_loop(0, qi, body, 0)
    step(qi, masked=True)

    lam = (jnp.exp(jnp.sum(lq1_ref[...] * lk1_ref[...], axis=-1, keepdims=True))
           - jnp.exp(jnp.sum(lq2_ref[...] * lk2_ref[...], axis=-1, keepdims=True))
           + lam_init)
    for h in heads:
        cols = slice(h * HEAD_W, (h + 1) * HEAD_W)
        acc = acc_ref[h]
        o_all = acc[0:HEAD_W, :] / acc[HEAD_W:HEAD_W + 1, :]
        o = (o_all[:, :tq] - lam * o_all[:, tq:]).T
        o = o * lax.rsqrt(jnp.mean(o * o, axis=-1, keepdims=True) + LN_EPS)
        o = o * subln_ref[...] * (1.0 - lam_init)
        o_ref[:, cols] = (o * _silu(g_ref[:, cols])).astype(BF16)


def _diff_attn(qz, k, vt, g_attn, lam_q1, lam_k1, lam_q2, lam_k2, subln_g, *, lam_init):
    B, S, _ = k.shape
    tq = SEQ_TILE
    nt = S // tq
    kern = functools.partial(_diff_attn_kernel, tq=tq, lam_init=lam_init)
    vec = lambda n: pl.BlockSpec((1, n), lambda b, i: (0, 0))
    return pl.pallas_call(
        kern,
        grid=(B, nt),
        in_specs=[
            vec(HEAD_DIM), vec(HEAD_DIM), vec(HEAD_DIM), vec(HEAD_DIM), vec(HEAD_W),
            pl.BlockSpec((None, None, ATTN_W, 2 * tq), lambda b, i: (b, i, 0, 0)),
            pl.BlockSpec((None, S, ATTN_W), lambda b, i: (b, 0, 0)),
            pl.BlockSpec((None, nt, N_HEADS * V_ROWS, tq), lambda b, i: (b, 0, 0, 0)),
            pl.BlockSpec((None, tq, ATTN_W), lambda b, i: (b, i, 0)),
        ],
        out_specs=pl.BlockSpec((None, tq, ATTN_W), lambda b, i: (b, i, 0)),
        out_shape=jax.ShapeDtypeStruct((B, S, ATTN_W), BF16),
        scratch_shapes=[
            pltpu.VMEM((N_HEADS, 1, 2 * tq), F32),
            pltpu.VMEM((N_HEADS, V_ROWS, 2 * tq), F32),
        ],
        compiler_params=pltpu.CompilerParams(
            dimension_semantics=("arbitrary", "arbitrary"),
            vmem_limit_bytes=V7X_VMEM_LIMIT_BYTES),
        name="diff_attn",
    )(lam_q1, lam_k1, lam_q2, lam_k2, subln_g, qz, k, vt, g_attn)


def _out_proj_kernel(x_ref, attn_ref, pool_ref, p_ref, wout_ref, lng_ref, lnb_ref,
                     wpe_ref, wpg_ref, bpg_ref, o_ref, *, alpha):
    mix = (jnp.dot(attn_ref[...], wout_ref[0:ATTN_W, :], preferred_element_type=F32)
           + jnp.dot(pool_ref[...], wout_ref[ATTN_W:, :], preferred_element_type=F32))
    r = alpha * x_ref[...] + mix
    mu = jnp.mean(r, axis=-1, keepdims=True)
    c = r - mu
    var = jnp.mean(c * c, axis=-1, keepdims=True)
    h = c * lax.rsqrt(var + LN_EPS) * lng_ref[...] + lnb_ref[...]
    logits = jnp.dot(h.astype(BF16), wpg_ref[...], preferred_element_type=F32) + bpg_ref[...]
    gate = 1.0 / (1.0 + jnp.exp(-logits))
    pe = jnp.dot(p_ref[...].astype(BF16), wpe_ref[...], preferred_element_type=F32)
    o_ref[...] = h + pe * gate


def _out_proj(x2, attn2, pool2, p2, w_out_bf, ln_g, ln_b, w_pe_bf, w_pg_bf, b_pg, *, alpha):
    R = x2.shape[0]
    tm = SEQ_TILE
    kern = functools.partial(_out_proj_kernel, alpha=alpha)
    rows = lambda n: pl.BlockSpec((tm, n), lambda i: (i, 0))
    full = lambda a, b: pl.BlockSpec((a, b), lambda i: (0, 0))
    return pl.pallas_call(
        kern,
        grid=(R // tm,),
        in_specs=[
            rows(D_MODEL), rows(ATTN_W), rows(POOL_W), rows(PLE_DIM),
            full(ATTN_W + POOL_W, D_MODEL), full(1, D_MODEL), full(1, D_MODEL),
            full(PLE_DIM, D_MODEL), full(D_MODEL, D_MODEL), full(1, D_MODEL),
        ],
        out_specs=rows(D_MODEL),
        out_shape=jax.ShapeDtypeStruct((R, D_MODEL), F32),
        compiler_params=pltpu.CompilerParams(
            dimension_semantics=("arbitrary",),
            vmem_limit_bytes=V7X_VMEM_LIMIT_BYTES),
        name="out_proj_ln_ple",
    )(x2, attn2, pool2, p2, w_out_bf, ln_g, ln_b, w_pe_bf, w_pg_bf, b_pg)


def kernel(x, p, w_in, lam_q1, lam_k1, lam_q2, lam_k2, subln_g, w_pool, pool_scale,
           w_out, ln_g, ln_b, w_pe, w_pg, b_pg):
    B, S, D = x.shape
    depth = w_in.shape[0]
    h = x
    for i in range(depth):
        alpha = (2.0 * depth) ** 0.25
        w = w_in[i]
        wqv_t = jnp.concatenate([w[:, 0:ATTN_W], w[:, 2 * ATTN_W:3 * ATTN_W]], axis=1).T.astype(BF16)
        w_rest = jnp.concatenate([w[:, ATTN_W:2 * ATTN_W], w[:, 3 * ATTN_W:]], axis=1).astype(BF16)
        qz, k, vt, g_attn, pool_mix = _in_proj_pool(
            h, wqv_t, w_rest, w_pool[i].astype(BF16), pool_scale[i][None, :])
        attn = _diff_attn(
            qz, k, vt, g_attn, lam_q1[i][None, :], lam_k1[i][None, :], lam_q2[i][None, :],
            lam_k2[i][None, :], subln_g[i][None, :], lam_init=_lambda_init(i))
        out = _out_proj(
            h.reshape(B * S, D), attn.reshape(B * S, ATTN_W), pool_mix.reshape(B * S, POOL_W),
            p[i].reshape(B * S, PLE_DIM), w_out[i].astype(BF16), ln_g[i][None, :],
            ln_b[i][None, :], w_pe[i].astype(BF16), w_pg[i].astype(BF16), b_pg[i][None, :],
            alpha=alpha)
        h = out.reshape(B, S, D)
    return h
```

```python
import functools
import math

import jax
import jax.numpy as jnp
from jax import lax
from jax.experimental import pallas as pl
from jax.experimental.pallas import tpu as pltpu

D_MODEL = 1024
PLE_DIM = 256
N_HEADS = 4
HEAD_DIM = 64
HEAD_W = 2 * HEAD_DIM
ATTN_W = N_HEADS * HEAD_W
N_POOL_GROUPS = 4
POOL_WINDOWS = (2, 4, 8, 16)
POOL_CH = 128
POOL_W = N_POOL_GROUPS * POOL_CH
REST_W = ATTN_W + ATTN_W + 2 * POOL_W
LN_EPS = 1e-5
POOL_HALO = max(POOL_WINDOWS)

BF16_SUBLANE_TILE = 16
V_ROWS = HEAD_W + BF16_SUBLANE_TILE

SEQ_TILE = 512
KEY_TILE = 512
QBLK = 512
V7X_VMEM_LIMIT_BYTES = 56 * 1024 * 1024

BF16 = jnp.bfloat16
F32 = jnp.float32
NT_DIMS = (((1,), (1,)), ((), ()))


def _lambda_init(layer_idx):
    return 0.8 - 0.6 * math.exp(-0.3 * layer_idx)


def _silu(g):
    return g / (1.0 + jnp.exp(-g))


def _in_proj_pool_kernel(x_ref, wqv_ref, wrest_ref, wpool_ref, pscale_ref,
                         qz_ref, k_ref, vt_ref, gattn_ref, pool_ref, ubuf_ref, *, tm, tk):
    i = pl.program_id(1)
    xb = x_ref[...].astype(BF16)

    scale = 1.0 / math.sqrt(HEAD_DIM)
    q_t = lax.dot_general(wqv_ref[0:ATTN_W, :], xb, NT_DIMS, preferred_element_type=F32)
    v_t = lax.dot_general(wqv_ref[ATTN_W:, :], xb, NT_DIMS, preferred_element_type=F32)
    zeros = jnp.zeros((HEAD_DIM, tm), BF16)
    ones_tile = jnp.where(
        lax.broadcasted_iota(jnp.int32, (BF16_SUBLANE_TILE, tk), 0) == 0, 1.0, 0.0).astype(BF16)
    for h in range(N_HEADS):
        f0 = h * HEAD_W
        qz_ref[f0:f0 + HEAD_DIM, 0:tm] = (q_t[f0:f0 + HEAD_DIM] * scale).astype(BF16)
        qz_ref[f0 + HEAD_DIM:f0 + HEAD_W, 0:tm] = zeros
        qz_ref[f0:f0 + HEAD_DIM, tm:] = zeros
        qz_ref[f0 + HEAD_DIM:f0 + HEAD_W, tm:] = (q_t[f0 + HEAD_DIM:f0 + HEAD_W] * scale).astype(BF16)
        r0 = h * V_ROWS
        for c in range(tm // tk):
            vt_ref[c, r0:r0 + HEAD_W, :] = v_t[f0:f0 + HEAD_W, c * tk:(c + 1) * tk].astype(BF16)
            vt_ref[c, r0 + HEAD_W:r0 + V_ROWS, :] = ones_tile

    def proj(lo, width):
        return jnp.dot(xb, wrest_ref[:, lo:lo + width], preferred_element_type=F32)

    k_ref[...] = proj(0, ATTN_W).astype(BF16)
    gattn_ref[...] = proj(ATTN_W, ATTN_W)

    @pl.when(i == 0)
    def _():
        ubuf_ref[0:POOL_HALO, :] = jnp.zeros((POOL_HALO, POOL_W), F32)

    @pl.when(i > 0)
    def _():
        ubuf_ref[0:POOL_HALO, :] = ubuf_ref[tm:tm + POOL_HALO, :]

    ubuf_ref[POOL_HALO:, :] = proj(2 * ATTN_W, POOL_W)
    g_pool = proj(2 * ATTN_W + POOL_W, POOL_W)

    t = (i * tm + lax.broadcasted_iota(jnp.int32, (tm, 1), 0) + 1).astype(F32)
    for g, w in enumerate(POOL_WINDOWS):
        cols = slice(g * POOL_CH, (g + 1) * POOL_CH)
        ext = ubuf_ref[:, cols]
        acc = ext
        span = 1
        while span < w:
            acc = acc + pltpu.roll(acc, span, axis=0)
            span *= 2
        u = ext[POOL_HALO:, :]
        mean = acc[POOL_HALO:, :] / jnp.minimum(t, float(w))
        d = (mean - u).astype(BF16)
        y = jnp.dot(d, wpool_ref[g], preferred_element_type=F32)
        y = y * pscale_ref[:, cols] * _silu(g_pool[:, cols])
        pool_ref[:, cols] = y.astype(BF16)


def _in_proj_pool(x, wqv_t, w_rest, w_pool_bf, pool_scale):
    B, S, _ = x.shape
    tm, tk = SEQ_TILE, KEY_TILE
    nt = S // tm
    kern = functools.partial(_in_proj_pool_kernel, tm=tm, tk=tk)
    return pl.pallas_call(
        kern,
        grid=(B, nt),
        in_specs=[
            pl.BlockSpec((None, tm, D_MODEL), lambda b, i: (b, i, 0)),
            pl.BlockSpec((2 * ATTN_W, D_MODEL), lambda b, i: (0, 0)),
            pl.BlockSpec((D_MODEL, REST_W), lambda b, i: (0, 0)),
            pl.BlockSpec((N_POOL_GROUPS, POOL_CH, POOL_CH), lambda b, i: (0, 0, 0)),
            pl.BlockSpec((1, POOL_W), lambda b, i: (0, 0)),
        ],
        out_specs=[
            pl.BlockSpec((None, None, ATTN_W, 2 * tm), lambda b, i: (b, i, 0, 0)),
            pl.BlockSpec((None, tm, ATTN_W), lambda b, i: (b, i, 0)),
            pl.BlockSpec((None, tm // tk, N_HEADS * V_ROWS, tk), lambda b, i: (b, i, 0, 0)),
            pl.BlockSpec((None, tm, ATTN_W), lambda b, i: (b, i, 0)),
            pl.BlockSpec((None, tm, POOL_W), lambda b, i: (b, i, 0)),
        ],
        out_shape=[
            jax.ShapeDtypeStruct((B, nt, ATTN_W, 2 * tm), BF16),
            jax.ShapeDtypeStruct((B, S, ATTN_W), BF16),
            jax.ShapeDtypeStruct((B, S // tk, N_HEADS * V_ROWS, tk), BF16),
            jax.ShapeDtypeStruct((B, S, ATTN_W), F32),
            jax.ShapeDtypeStruct((B, S, POOL_W), BF16),
        ],
        scratch_shapes=[pltpu.VMEM((POOL_HALO + tm, POOL_W), F32)],
        compiler_params=pltpu.CompilerParams(
            dimension_semantics=("arbitrary", "arbitrary"),
            vmem_limit_bytes=V7X_VMEM_LIMIT_BYTES),
        name="in_proj_pool",
    )(x, wqv_t, w_rest, w_pool_bf, pool_scale)


def _diff_attn_kernel(lq1_ref, lk1_ref, lq2_ref, lk2_ref, subln_ref,
                      qz_ref, k_ref, vt_ref, g_ref, o_ref,
                      m_ref, acc_ref, s_ref, *, tq, tk, lam_init):
    qi = pl.program_id(1)
    heads = range(N_HEADS)

    m_ref[...] = jnp.full(m_ref.shape, -jnp.inf, F32)
    acc_ref[...] = jnp.zeros(acc_ref.shape, F32)

    n_blk = 2 * tq // QBLK
    units = [(h, n) for h in heads for n in range(n_blk)]

    def scores(j, u, diag_offset):
        h, n = units[u]
        qcols = slice(n * QBLK, (n + 1) * QBLK)
        k_start = pl.multiple_of(j * tk, tk)
        kj = k_ref[pl.ds(k_start, tk), h * HEAD_W:(h + 1) * HEAD_W]
        s_t = jnp.dot(kj, qz_ref[h * HEAD_W:(h + 1) * HEAD_W, qcols],
                      preferred_element_type=F32)
        if diag_offset is not None:
            key = diag_offset + lax.broadcasted_iota(jnp.int32, (tk, QBLK), 0)
            qry = (n * QBLK) % tq + lax.broadcasted_iota(jnp.int32, (tk, QBLK), 1)
            s_t = jnp.where(key <= qry, s_t, -jnp.inf)
        m_prev = m_ref[h, :, qcols]
        m_new = jnp.maximum(m_prev, jnp.max(s_t, axis=0, keepdims=True))
        m_ref[h, :, qcols] = m_new
        s_ref[u % 2] = s_t
        return m_prev, m_new

    def accumulate(j, u, m_prev, m_new):
        h, n = units[u]
        qcols = slice(n * QBLK, (n + 1) * QBLK)
        vj = vt_ref[j, h * V_ROWS:(h + 1) * V_ROWS, :]
        alpha = jnp.exp(m_prev - m_new)
        p_t = jnp.exp(s_ref[u % 2] - m_new).astype(BF16)
        acc_ref[h, :, qcols] = (alpha * acc_ref[h, :, qcols]
                                + jnp.dot(vj, p_t, preferred_element_type=F32))

    def step(j, diag_offset=None):
        pending = scores(j, 0, diag_offset)
        for u in range(len(units)):
            nxt = scores(j, u + 1, diag_offset) if u + 1 < len(units) else None
            accumulate(j, u, *pending)
            pending = nxt

    def body(j, carry):
        step(j)
        return carry

    steps_per_tile = tq // tk
    lax.fori_loop(0, qi * steps_per_tile, body, 0)
    for c in range(steps_per_tile):
        step(qi * steps_per_tile + c, diag_offset=c * tk)

    lam = (jnp.exp(jnp.sum(lq1_ref[...] * lk1_ref[...], axis=-1, keepdims=True))
           - jnp.exp(jnp.sum(lq2_ref[...] * lk2_ref[...], axis=-1, keepdims=True))
           + lam_init)
    for h in heads:
        cols = slice(h * HEAD_W, (h + 1) * HEAD_W)
        acc = acc_ref[h]
        o_all = acc[0:HEAD_W, :] / acc[HEAD_W:HEAD_W + 1, :]
        o = (o_all[:, :tq] - lam * o_all[:, tq:]).T
        o = o * lax.rsqrt(jnp.mean(o * o, axis=-1, keepdims=True) + LN_EPS)
        o = o * subln_ref[...] * (1.0 - lam_init)
        o_ref[:, cols] = (o * _silu(g_ref[:, cols])).astype(BF16)


def _diff_attn(qz, k, vt, g_attn, lam_q1, lam_k1, lam_q2, lam_k2, subln_g, *, lam_init):
    B, S, _ = k.shape
    tq, tk = SEQ_TILE, KEY_TILE
    nt = S // tq
    kern = functools.partial(_diff_attn_kernel, tq=tq, tk=tk, lam_init=lam_init)
    vec = lambda n: pl.BlockSpec((1, n), lambda b, i: (0, 0))
    return pl.pallas_call(
        kern,
        grid=(B, nt),
        in_specs=[
            vec(HEAD_DIM), vec(HEAD_DIM), vec(HEAD_DIM), vec(HEAD_DIM), vec(HEAD_W),
            pl.BlockSpec((None, None, ATTN_W, 2 * tq), lambda b, i: (b, i, 0, 0)),
            pl.BlockSpec((None, S, ATTN_W), lambda b, i: (b, 0, 0)),
            pl.BlockSpec((None, S // tk, N_HEADS * V_ROWS, tk), lambda b, i: (b, 0, 0, 0)),
            pl.BlockSpec((None, tq, ATTN_W), lambda b, i: (b, i, 0)),
        ],
        out_specs=pl.BlockSpec((None, tq, ATTN_W), lambda b, i: (b, i, 0)),
        out_shape=jax.ShapeDtypeStruct((B, S, ATTN_W), BF16),
        scratch_shapes=[
            pltpu.VMEM((N_HEADS, 1, 2 * tq), F32),
            pltpu.VMEM((N_HEADS, V_ROWS, 2 * tq), F32),
            pltpu.VMEM((2, tk, QBLK), F32),
        ],
        compiler_params=pltpu.CompilerParams(
            dimension_semantics=("arbitrary", "arbitrary"),
            vmem_limit_bytes=V7X_VMEM_LIMIT_BYTES),
        name="diff_attn",
    )(lam_q1, lam_k1, lam_q2, lam_k2, subln_g, qz, k, vt, g_attn)


def _out_proj_kernel(x_ref, attn_ref, pool_ref, p_ref, wout_ref, lng_ref, lnb_ref,
                     wpe_ref, wpg_ref, bpg_ref, o_ref, *, alpha):
    mix = (jnp.dot(attn_ref[...], wout_ref[0:ATTN_W, :], preferred_element_type=F32)
           + jnp.dot(pool_ref[...], wout_ref[ATTN_W:, :], preferred_element_type=F32))
    r = alpha * x_ref[...] + mix
    mu = jnp.mean(r, axis=-1, keepdims=True)
    c = r - mu
    var = jnp.mean(c * c, axis=-1, keepdims=True)
    h = c * lax.rsqrt(var + LN_EPS) * lng_ref[...] + lnb_ref[...]
    logits = jnp.dot(h.astype(BF16), wpg_ref[...], preferred_element_type=F32) + bpg_ref[...]
    gate = 1.0 / (1.0 + jnp.exp(-logits))
    pe = jnp.dot(p_ref[...].astype(BF16), wpe_ref[...], preferred_element_type=F32)
    o_ref[...] = h + pe * gate


def _out_proj(x2, attn2, pool2, p2, w_out_bf, ln_g, ln_b, w_pe_bf, w_pg_bf, b_pg, *, alpha):
    R = x2.shape[0]
    tm = SEQ_TILE
    kern = functools.partial(_out_proj_kernel, alpha=alpha)
    rows = lambda n: pl.BlockSpec((tm, n), lambda i: (i, 0))
    full = lambda a, b: pl.BlockSpec((a, b), lambda i: (0, 0))
    return pl.pallas_call(
        kern,
        grid=(R // tm,),
        in_specs=[
            rows(D_MODEL), rows(ATTN_W), rows(POOL_W), rows(PLE_DIM),
            full(ATTN_W + POOL_W, D_MODEL), full(1, D_MODEL), full(1, D_MODEL),
            full(PLE_DIM, D_MODEL), full(D_MODEL, D_MODEL), full(1, D_MODEL),
        ],
        out_specs=rows(D_MODEL),
        out_shape=jax.ShapeDtypeStruct((R, D_MODEL), F32),
        compiler_params=pltpu.CompilerParams(
            dimension_semantics=("arbitrary",),
            vmem_limit_bytes=V7X_VMEM_LIMIT_BYTES),
        name="out_proj_ln_ple",
    )(x2, attn2, pool2, p2, w_out_bf, ln_g, ln_b, w_pe_bf, w_pg_bf, b_pg)


def kernel(x, p, w_in, lam_q1, lam_k1, lam_q2, lam_k2, subln_g, w_pool, pool_scale,
           w_out, ln_g, ln_b, w_pe, w_pg, b_pg):
    B, S, D = x.shape
    depth = w_in.shape[0]
    h = x
    for i in range(depth):
        alpha = (2.0 * depth) ** 0.25
        w = w_in[i]
        wqv_t = jnp.concatenate([w[:, 0:ATTN_W], w[:, 2 * ATTN_W:3 * ATTN_W]], axis=1).T.astype(BF16)
        w_rest = jnp.concatenate([w[:, ATTN_W:2 * ATTN_W], w[:, 3 * ATTN_W:]], axis=1).astype(BF16)
        qz, k, vt, g_attn, pool_mix = _in_proj_pool(
            h, wqv_t, w_rest, w_pool[i].astype(BF16), pool_scale[i][None, :])
        attn = _diff_attn(
            qz, k, vt, g_attn, lam_q1[i][None, :], lam_k1[i][None, :], lam_q2[i][None, :],
            lam_k2[i][None, :], subln_g[i][None, :], lam_init=_lambda_init(i))
        out = _out_proj(
            h.reshape(B * S, D), attn.reshape(B * S, ATTN_W), pool_mix.reshape(B * S, POOL_W),
            p[i].reshape(B * S, PLE_DIM), w_out[i].astype(BF16), ln_g[i][None, :],
            ln_b[i][None, :], w_pe[i].astype(BF16), w_pg[i].astype(BF16), b_pg[i][None, :],
            alpha=alpha)
        h = out.reshape(B, S, D)
    return h
```

```python
import functools
import math

import jax
import jax.numpy as jnp
from jax import lax
from jax.experimental import pallas as pl
from jax.experimental.pallas import tpu as pltpu

D_MODEL = 1024
PLE_DIM = 256
N_HEADS = 4
HEAD_DIM = 64
HEAD_W = 2 * HEAD_DIM
ATTN_W = N_HEADS * HEAD_W
N_POOL_GROUPS = 4
POOL_WINDOWS = (2, 4, 8, 16)
POOL_CH = 128
POOL_W = N_POOL_GROUPS * POOL_CH
REST_W = ATTN_W + ATTN_W + 2 * POOL_W
LN_EPS = 1e-5
POOL_HALO = max(POOL_WINDOWS)

BF16_SUBLANE_TILE = 16
V_ROWS = HEAD_W + BF16_SUBLANE_TILE

SEQ_TILE = 512
KEY_TILE = 512
QBLK = 512
V7X_VMEM_LIMIT_BYTES = 56 * 1024 * 1024

BF16 = jnp.bfloat16
F32 = jnp.float32
NT_DIMS = (((1,), (1,)), ((), ()))


def _lambda_init(layer_idx):
    return 0.8 - 0.6 * math.exp(-0.3 * layer_idx)


def _silu(g):
    return g / (1.0 + jnp.exp(-g))


def _in_proj_pool_kernel(x_ref, wqv_ref, wrest_ref, wpool_ref, pscale_ref,
                         qz_ref, k_ref, vt_ref, gattn_ref, pool_ref, ubuf_ref, *, tm, tk):
    i = pl.program_id(1)
    xb = x_ref[...].astype(BF16)

    scale = 1.0 / math.sqrt(HEAD_DIM)
    q_t = lax.dot_general(wqv_ref[0:ATTN_W, :], xb, NT_DIMS, preferred_element_type=F32)
    v_t = lax.dot_general(wqv_ref[ATTN_W:, :], xb, NT_DIMS, preferred_element_type=F32)
    zeros = jnp.zeros((HEAD_DIM, tm), BF16)
    ones_tile = jnp.where(
        lax.broadcasted_iota(jnp.int32, (BF16_SUBLANE_TILE, tk), 0) == 0, 1.0, 0.0).astype(BF16)
    for h in range(N_HEADS):
        f0 = h * HEAD_W
        qz_ref[f0:f0 + HEAD_DIM, 0:tm] = (q_t[f0:f0 + HEAD_DIM] * scale).astype(BF16)
        qz_ref[f0 + HEAD_DIM:f0 + HEAD_W, 0:tm] = zeros
        qz_ref[f0:f0 + HEAD_DIM, tm:] = zeros
        qz_ref[f0 + HEAD_DIM:f0 + HEAD_W, tm:] = (q_t[f0 + HEAD_DIM:f0 + HEAD_W] * scale).astype(BF16)
        r0 = h * V_ROWS
        for c in range(tm // tk):
            vt_ref[c, r0:r0 + HEAD_W, :] = v_t[f0:f0 + HEAD_W, c * tk:(c + 1) * tk].astype(BF16)
            vt_ref[c, r0 + HEAD_W:r0 + V_ROWS, :] = ones_tile

    def proj(lo, width):
        return jnp.dot(xb, wrest_ref[:, lo:lo + width], preferred_element_type=F32)

    k_ref[...] = proj(0, ATTN_W).astype(BF16)
    gattn_ref[...] = proj(ATTN_W, ATTN_W)

    @pl.when(i == 0)
    def _():
        ubuf_ref[0:POOL_HALO, :] = jnp.zeros((POOL_HALO, POOL_W), F32)

    @pl.when(i > 0)
    def _():
        ubuf_ref[0:POOL_HALO, :] = ubuf_ref[tm:tm + POOL_HALO, :]

    ubuf_ref[POOL_HALO:, :] = proj(2 * ATTN_W, POOL_W)
    g_pool = proj(2 * ATTN_W + POOL_W, POOL_W)

    t = (i * tm + lax.broadcasted_iota(jnp.int32, (tm, 1), 0) + 1).astype(F32)
    for g, w in enumerate(POOL_WINDOWS):
        cols = slice(g * POOL_CH, (g + 1) * POOL_CH)
        ext = ubuf_ref[:, cols]
        acc = ext
        span = 1
        while span < w:
            acc = acc + pltpu.roll(acc, span, axis=0)
            span *= 2
        u = ext[POOL_HALO:, :]
        mean = acc[POOL_HALO:, :] / jnp.minimum(t, float(w))
        d = (mean - u).astype(BF16)
        y = jnp.dot(d, wpool_ref[g], preferred_element_type=F32)
        y = y * pscale_ref[:, cols] * _silu(g_pool[:, cols])
        pool_ref[:, cols] = y.astype(BF16)


def _in_proj_pool(x, wqv_t, w_rest, w_pool_bf, pool_scale):
    B, S, _ = x.shape
    tm, tk = SEQ_TILE, KEY_TILE
    nt = S // tm
    kern = functools.partial(_in_proj_pool_kernel, tm=tm, tk=tk)
    return pl.pallas_call(
        kern,
        grid=(B, nt),
        in_specs=[
            pl.BlockSpec((None, tm, D_MODEL), lambda b, i: (b, i, 0)),
            pl.BlockSpec((2 * ATTN_W, D_MODEL), lambda b, i: (0, 0)),
            pl.BlockSpec((D_MODEL, REST_W), lambda b, i: (0, 0)),
            pl.BlockSpec((N_POOL_GROUPS, POOL_CH, POOL_CH), lambda b, i: (0, 0, 0)),
            pl.BlockSpec((1, POOL_W), lambda b, i: (0, 0)),
        ],
        out_specs=[
            pl.BlockSpec((None, None, ATTN_W, 2 * tm), lambda b, i: (b, i, 0, 0)),
            pl.BlockSpec((None, tm, ATTN_W), lambda b, i: (b, i, 0)),
            pl.BlockSpec((None, tm // tk, N_HEADS * V_ROWS, tk), lambda b, i: (b, i, 0, 0)),
            pl.BlockSpec((None, tm, ATTN_W), lambda b, i: (b, i, 0)),
            pl.BlockSpec((None, tm, POOL_W), lambda b, i: (b, i, 0)),
        ],
        out_shape=[
            jax.ShapeDtypeStruct((B, nt, ATTN_W, 2 * tm), BF16),
            jax.ShapeDtypeStruct((B, S, ATTN_W), BF16),
            jax.ShapeDtypeStruct((B, S // tk, N_HEADS * V_ROWS, tk), BF16),
            jax.ShapeDtypeStruct((B, S, ATTN_W), F32),
            jax.ShapeDtypeStruct((B, S, POOL_W), BF16),
        ],
        scratch_shapes=[pltpu.VMEM((POOL_HALO + tm, POOL_W), F32)],
        compiler_params=pltpu.CompilerParams(
            dimension_semantics=("arbitrary", "arbitrary"),
            vmem_limit_bytes=V7X_VMEM_LIMIT_BYTES),
        name="in_proj_pool",
    )(x, wqv_t, w_rest, w_pool_bf, pool_scale)


def _diff_attn_kernel(lq1_ref, lk1_ref, lq2_ref, lk2_ref, subln_ref,
                      qz_ref, k_ref, vt_ref, g_ref, o_ref,
                      m_ref, acc_ref, s_ref, bias_ref, *, tq, tk, lam_init):
    qi = pl.program_id(1)
    heads = range(N_HEADS)

    @pl.when((pl.program_id(0) == 0) & (qi == 0))
    def _():
        key = lax.broadcasted_iota(jnp.int32, (tk, QBLK), 0)
        qry = lax.broadcasted_iota(jnp.int32, (tk, QBLK), 1)
        bias_ref[0] = jnp.zeros((tk, QBLK), F32)
        bias_ref[1] = jnp.where(key <= qry, 0.0, -jnp.inf).astype(F32)

    m_ref[...] = jnp.full(m_ref.shape, -jnp.inf, F32)
    acc_ref[...] = jnp.zeros(acc_ref.shape, F32)

    units = [(h, n) for h in heads for n in range(2)]
    n_units = len(units)

    def scores(j, u, bias_slot):
        h, n = units[u]
        qcols = slice(n * QBLK, (n + 1) * QBLK)
        k_start = pl.multiple_of(j * tk, tk)
        kj = k_ref[pl.ds(k_start, tk), h * HEAD_W:(h + 1) * HEAD_W]
        s_t = jnp.dot(kj, qz_ref[h * HEAD_W:(h + 1) * HEAD_W, qcols],
                      preferred_element_type=F32)
        if bias_slot is not None:
            s_t = s_t + bias_ref[bias_slot]
        m_prev = m_ref[h, :, qcols]
        m_new = jnp.maximum(m_prev, jnp.max(s_t, axis=0, keepdims=True))
        m_ref[h, :, qcols] = m_new
        s_ref[u % 2] = s_t
        return m_prev, m_new

    def accumulate(j, u, m_prev, m_new):
        h, n = units[u]
        qcols = slice(n * QBLK, (n + 1) * QBLK)
        vj = vt_ref[j, h * V_ROWS:(h + 1) * V_ROWS, :]
        alpha = jnp.exp(m_prev - m_new)
        p_t = jnp.exp(s_ref[u % 2] - m_new).astype(BF16)
        acc_ref[h, :, qcols] = (alpha * acc_ref[h, :, qcols]
                                + jnp.dot(vj, p_t, preferred_element_type=F32))

    def diag_slot(j):
        return (j == qi).astype(jnp.int32)

    def body(j, pending):
        for u in range(n_units):
            if u + 1 < n_units:
                nxt = scores(j, u + 1, None)
            else:
                nxt = scores(j + 1, 0, diag_slot(j + 1))
            accumulate(j, u, *pending)
            pending = nxt
        return pending

    pending = scores(0, 0, diag_slot(0))
    pending = lax.fori_loop(0, qi, body, pending)
    for u in range(n_units):
        nxt = scores(qi, u + 1, 1) if u + 1 < n_units else None
        accumulate(qi, u, *pending)
        pending = nxt

    lam = (jnp.exp(jnp.sum(lq1_ref[...] * lk1_ref[...], axis=-1, keepdims=True))
           - jnp.exp(jnp.sum(lq2_ref[...] * lk2_ref[...], axis=-1, keepdims=True))
           + lam_init)
    for h in heads:
        cols = slice(h * HEAD_W, (h + 1) * HEAD_W)
        acc = acc_ref[h]
        o_all = acc[0:HEAD_W, :] / acc[HEAD_W:HEAD_W + 1, :]
        o = (o_all[:, :tq] - lam * o_all[:, tq:]).T
        o = o * lax.rsqrt(jnp.mean(o * o, axis=-1, keepdims=True) + LN_EPS)
        o = o * subln_ref[...] * (1.0 - lam_init)
        o_ref[:, cols] = (o * _silu(g_ref[:, cols])).astype(BF16)


def _diff_attn(qz, k, vt, g_attn, lam_q1, lam_k1, lam_q2, lam_k2, subln_g, *, lam_init):
    B, S, _ = k.shape
    tq, tk = SEQ_TILE, KEY_TILE
    assert tq == tk == QBLK, "one causal bias tile serves every unit of the diagonal tile"
    nt = S // tq
    kern = functools.partial(_diff_attn_kernel, tq=tq, tk=tk, lam_init=lam_init)
    vec = lambda n: pl.BlockSpec((1, n), lambda b, i: (0, 0))
    return pl.pallas_call(
        kern,
        grid=(B, nt),
        in_specs=[
            vec(HEAD_DIM), vec(HEAD_DIM), vec(HEAD_DIM), vec(HEAD_DIM), vec(HEAD_W),
            pl.BlockSpec((None, None, ATTN_W, 2 * tq), lambda b, i: (b, i, 0, 0)),
            pl.BlockSpec((None, S, ATTN_W), lambda b, i: (b, 0, 0)),
            pl.BlockSpec((None, S // tk, N_HEADS * V_ROWS, tk), lambda b, i: (b, 0, 0, 0)),
            pl.BlockSpec((None, tq, ATTN_W), lambda b, i: (b, i, 0)),
        ],
        out_specs=pl.BlockSpec((None, tq, ATTN_W), lambda b, i: (b, i, 0)),
        out_shape=jax.ShapeDtypeStruct((B, S, ATTN_W), BF16),
        scratch_shapes=[
            pltpu.VMEM((N_HEADS, 1, 2 * tq), F32),
            pltpu.VMEM((N_HEADS, V_ROWS, 2 * tq), F32),
            pltpu.VMEM((2, tk, QBLK), F32),
            pltpu.VMEM((2, tk, QBLK), F32),
        ],
        compiler_params=pltpu.CompilerParams(
            dimension_semantics=("arbitrary", "arbitrary"),
            vmem_limit_bytes=V7X_VMEM_LIMIT_BYTES),
        name="diff_attn",
    )(lam_q1, lam_k1, lam_q2, lam_k2, subln_g, qz, k, vt, g_attn)


def _out_proj_kernel(x_ref, attn_ref, pool_ref, p_ref, wout_ref, lng_ref, lnb_ref,
                     wpe_ref, wpg_ref, bpg_ref, o_ref, *, alpha):
    mix = (jnp.dot(attn_ref[...], wout_ref[0:ATTN_W, :], preferred_element_type=F32)
           + jnp.dot(pool_ref[...], wout_ref[ATTN_W:, :], preferred_element_type=F32))
    r = alpha * x_ref[...] + mix
    mu = jnp.mean(r, axis=-1, keepdims=True)
    c = r - mu
    var = jnp.mean(c * c, axis=-1, keepdims=True)
    h = c * lax.rsqrt(var + LN_EPS) * lng_ref[...] + lnb_ref[...]
    logits = jnp.dot(h.astype(BF16), wpg_ref[...], preferred_element_type=F32) + bpg_ref[...]
    gate = 1.0 / (1.0 + jnp.exp(-logits))
    pe = jnp.dot(p_ref[...].astype(BF16), wpe_ref[...], preferred_element_type=F32)
    o_ref[...] = h + pe * gate


def _out_proj(x2, attn2, pool2, p2, w_out_bf, ln_g, ln_b, w_pe_bf, w_pg_bf, b_pg, *, alpha):
    R = x2.shape[0]
    tm = SEQ_TILE
    kern = functools.partial(_out_proj_kernel, alpha=alpha)
    rows = lambda n: pl.BlockSpec((tm, n), lambda i: (i, 0))
    full = lambda a, b: pl.BlockSpec((a, b), lambda i: (0, 0))
    return pl.pallas_call(
        kern,
        grid=(R // tm,),
        in_specs=[
            rows(D_MODEL), rows(ATTN_W), rows(POOL_W), rows(PLE_DIM),
            full(ATTN_W + POOL_W, D_MODEL), full(1, D_MODEL), full(1, D_MODEL),
            full(PLE_DIM, D_MODEL), full(D_MODEL, D_MODEL), full(1, D_MODEL),
        ],
        out_specs=rows(D_MODEL),
        out_shape=jax.ShapeDtypeStruct((R, D_MODEL), F32),
        compiler_params=pltpu.CompilerParams(
            dimension_semantics=("arbitrary",),
            vmem_limit_bytes=V7X_VMEM_LIMIT_BYTES),
        name="out_proj_ln_ple",
    )(x2, attn2, pool2, p2, w_out_bf, ln_g, ln_b, w_pe_bf, w_pg_bf, b_pg)


def kernel(x, p, w_in, lam_q1, lam_k1, lam_q2, lam_k2, subln_g, w_pool, pool_scale,
           w_out, ln_g, ln_b, w_pe, w_pg, b_pg):
    B, S, D = x.shape
    depth = w_in.shape[0]
    h = x
    for i in range(depth):
        alpha = (2.0 * depth) ** 0.25
        w = w_in[i]
        wqv_t = jnp.concatenate([w[:, 0:ATTN_W], w[:, 2 * ATTN_W:3 * ATTN_W]], axis=1).T.astype(BF16)
        w_rest = jnp.concatenate([w[:, ATTN_W:2 * ATTN_W], w[:, 3 * ATTN_W:]], axis=1).astype(BF16)
        qz, k, vt, g_attn, pool_mix = _in_proj_pool(
            h, wqv_t, w_rest, w_pool[i].astype(BF16), pool_scale[i][None, :])
        attn = _diff_attn(
            qz, k, vt, g_attn, lam_q1[i][None, :], lam_k1[i][None, :], lam_q2[i][None, :],
            lam_k2[i][None, :], subln_g[i][None, :], lam_init=_lambda_init(i))
        out = _out_proj(
            h.reshape(B * S, D), attn.reshape(B * S, ATTN_W), pool_mix.reshape(B * S, POOL_W),
            p[i].reshape(B * S, PLE_DIM), w_out[i].astype(BF16), ln_g[i][None, :],
            ln_b[i][None, :], w_pe[i].astype(BF16), w_pg[i].astype(BF16), b_pg[i][None, :],
            alpha=alpha)
        h = out.reshape(B, S, D)
    return h
```

```python
import functools
import math

import jax
import jax.numpy as jnp
from jax import lax
from jax.experimental import pallas as pl
from jax.experimental.pallas import tpu as pltpu

D_MODEL = 1024
PLE_DIM = 256
N_HEADS = 4
HEAD_DIM = 64
HEAD_W = 2 * HEAD_DIM
ATTN_W = N_HEADS * HEAD_W
N_POOL_GROUPS = 4
POOL_WINDOWS = (2, 4, 8, 16)
POOL_CH = 128
POOL_W = N_POOL_GROUPS * POOL_CH
REST_W = ATTN_W + ATTN_W + 2 * POOL_W
LN_EPS = 1e-5
LOG2_E = math.log2(math.e)
POOL_HALO = max(POOL_WINDOWS)

BF16_SUBLANE_TILE = 16
V_ROWS = HEAD_W + BF16_SUBLANE_TILE

SEQ_TILE = 512
KEY_TILE = 512
QBLK = 512
V7X_VMEM_LIMIT_BYTES = 56 * 1024 * 1024

BF16 = jnp.bfloat16
F32 = jnp.float32
NT_DIMS = (((1,), (1,)), ((), ()))


def _lambda_init(layer_idx):
    return 0.8 - 0.6 * math.exp(-0.3 * layer_idx)


def _silu(g):
    return g / (1.0 + jnp.exp(-g))


def _in_proj_pool_kernel(x_ref, wqv_ref, wrest_ref, wpool_ref, pscale_ref,
                         qz_ref, k_ref, vt_ref, gattn_ref, pool_ref, ubuf_ref, *, tm, tk):
    i = pl.program_id(1)
    xb = x_ref[...].astype(BF16)

    scale = LOG2_E / math.sqrt(HEAD_DIM)
    q_t = lax.dot_general(wqv_ref[0:ATTN_W, :], xb, NT_DIMS, preferred_element_type=F32)
    v_t = lax.dot_general(wqv_ref[ATTN_W:, :], xb, NT_DIMS, preferred_element_type=F32)
    zeros = jnp.zeros((HEAD_DIM, tm), BF16)
    ones_tile = jnp.where(
        lax.broadcasted_iota(jnp.int32, (BF16_SUBLANE_TILE, tk), 0) == 0, 1.0, 0.0).astype(BF16)
    for h in range(N_HEADS):
        f0 = h * HEAD_W
        qz_ref[f0:f0 + HEAD_DIM, 0:tm] = (q_t[f0:f0 + HEAD_DIM] * scale).astype(BF16)
        qz_ref[f0 + HEAD_DIM:f0 + HEAD_W, 0:tm] = zeros
        qz_ref[f0:f0 + HEAD_DIM, tm:] = zeros
        qz_ref[f0 + HEAD_DIM:f0 + HEAD_W, tm:] = (q_t[f0 + HEAD_DIM:f0 + HEAD_W] * scale).astype(BF16)
        r0 = h * V_ROWS
        for c in range(tm // tk):
            vt_ref[c, r0:r0 + HEAD_W, :] = v_t[f0:f0 + HEAD_W, c * tk:(c + 1) * tk].astype(BF16)
            vt_ref[c, r0 + HEAD_W:r0 + V_ROWS, :] = ones_tile

    def proj(lo, width):
        return jnp.dot(xb, wrest_ref[:, lo:lo + width], preferred_element_type=F32)

    k_ref[...] = proj(0, ATTN_W).astype(BF16)
    gattn_ref[...] = proj(ATTN_W, ATTN_W)

    @pl.when(i == 0)
    def _():
        ubuf_ref[0:POOL_HALO, :] = jnp.zeros((POOL_HALO, POOL_W), F32)

    @pl.when(i > 0)
    def _():
        ubuf_ref[0:POOL_HALO, :] = ubuf_ref[tm:tm + POOL_HALO, :]

    ubuf_ref[POOL_HALO:, :] = proj(2 * ATTN_W, POOL_W)
    g_pool = proj(2 * ATTN_W + POOL_W, POOL_W)

    t = (i * tm + lax.broadcasted_iota(jnp.int32, (tm, 1), 0) + 1).astype(F32)
    for g, w in enumerate(POOL_WINDOWS):
        cols = slice(g * POOL_CH, (g + 1) * POOL_CH)
        ext = ubuf_ref[:, cols]
        acc = ext
        span = 1
        while span < w:
            acc = acc + pltpu.roll(acc, span, axis=0)
            span *= 2
        u = ext[POOL_HALO:, :]
        mean = acc[POOL_HALO:, :] / jnp.minimum(t, float(w))
        d = (mean - u).astype(BF16)
        y = jnp.dot(d, wpool_ref[g], preferred_element_type=F32)
        y = y * pscale_ref[:, cols] * _silu(g_pool[:, cols])
        pool_ref[:, cols] = y.astype(BF16)


def _in_proj_pool(x, wqv_t, w_rest, w_pool_bf, pool_scale):
    B, S, _ = x.shape
    tm, tk = SEQ_TILE, KEY_TILE
    nt = S // tm
    kern = functools.partial(_in_proj_pool_kernel, tm=tm, tk=tk)
    return pl.pallas_call(
        kern,
        grid=(B, nt),
        in_specs=[
            pl.BlockSpec((None, tm, D_MODEL), lambda b, i: (b, i, 0)),
            pl.BlockSpec((2 * ATTN_W, D_MODEL), lambda b, i: (0, 0)),
            pl.BlockSpec((D_MODEL, REST_W), lambda b, i: (0, 0)),
            pl.BlockSpec((N_POOL_GROUPS, POOL_CH, POOL_CH), lambda b, i: (0, 0, 0)),
            pl.BlockSpec((1, POOL_W), lambda b, i: (0, 0)),
        ],
        out_specs=[
            pl.BlockSpec((None, None, ATTN_W, 2 * tm), lambda b, i: (b, i, 0, 0)),
            pl.BlockSpec((None, tm, ATTN_W), lambda b, i: (b, i, 0)),
            pl.BlockSpec((None, tm // tk, N_HEADS * V_ROWS, tk), lambda b, i: (b, i, 0, 0)),
            pl.BlockSpec((None, tm, ATTN_W), lambda b, i: (b, i, 0)),
            pl.BlockSpec((None, tm, POOL_W), lambda b, i: (b, i, 0)),
        ],
        out_shape=[
            jax.ShapeDtypeStruct((B, nt, ATTN_W, 2 * tm), BF16),
            jax.ShapeDtypeStruct((B, S, ATTN_W), BF16),
            jax.ShapeDtypeStruct((B, S // tk, N_HEADS * V_ROWS, tk), BF16),
            jax.ShapeDtypeStruct((B, S, ATTN_W), F32),
            jax.ShapeDtypeStruct((B, S, POOL_W), BF16),
        ],
        scratch_shapes=[pltpu.VMEM((POOL_HALO + tm, POOL_W), F32)],
        compiler_params=pltpu.CompilerParams(
            dimension_semantics=("arbitrary", "arbitrary"),
            vmem_limit_bytes=V7X_VMEM_LIMIT_BYTES),
        name="in_proj_pool",
    )(x, wqv_t, w_rest, w_pool_bf, pool_scale)


def _diff_attn_kernel(lq1_ref, lk1_ref, lq2_ref, lk2_ref, subln_ref,
                      qz_ref, k_ref, vt_ref, g_ref, o_ref,
                      m_ref, acc_ref, s_ref, bias_ref, *, tq, tk, lam_init):
    qi = pl.program_id(1)
    heads = range(N_HEADS)

    @pl.when((pl.program_id(0) == 0) & (qi == 0))
    def _():
        key = lax.broadcasted_iota(jnp.int32, (tk, QBLK), 0)
        qry = lax.broadcasted_iota(jnp.int32, (tk, QBLK), 1)
        bias_ref[0] = jnp.zeros((tk, QBLK), F32)
        bias_ref[1] = jnp.where(key <= qry, 0.0, -jnp.inf).astype(F32)

    m_ref[...] = jnp.full(m_ref.shape, -jnp.inf, F32)
    acc_ref[...] = jnp.zeros(acc_ref.shape, F32)

    units = [(h, n) for h in heads for n in range(2)]
    n_units = len(units)

    def scores(j, u, bias_slot):
        h, n = units[u]
        qcols = slice(n * QBLK, (n + 1) * QBLK)
        k_start = pl.multiple_of(j * tk, tk)
        kj = k_ref[pl.ds(k_start, tk), h * HEAD_W:(h + 1) * HEAD_W]
        s_t = jnp.dot(kj, qz_ref[h * HEAD_W:(h + 1) * HEAD_W, qcols],
                      preferred_element_type=F32)
        if bias_slot is not None:
            s_t = s_t + bias_ref[bias_slot]
        m_prev = m_ref[h, :, qcols]
        m_new = jnp.maximum(m_prev, jnp.max(s_t, axis=0, keepdims=True))
        m_ref[h, :, qcols] = m_new
        s_ref[u % 2] = s_t
        return m_prev, m_new

    def accumulate(j, u, m_prev, m_new):
        h, n = units[u]
        qcols = slice(n * QBLK, (n + 1) * QBLK)
        vj = vt_ref[j, h * V_ROWS:(h + 1) * V_ROWS, :]
        alpha = jnp.exp2(m_prev - m_new)
        p_t = jnp.exp2(s_ref[u % 2] - m_new).astype(BF16)
        acc_ref[h, :, qcols] = (alpha * acc_ref[h, :, qcols]
                                + jnp.dot(vj, p_t, preferred_element_type=F32))

    def diag_slot(j):
        return (j == qi).astype(jnp.int32)

    def body(j, pending):
        for u in range(n_units):
            if u + 1 < n_units:
                nxt = scores(j, u + 1, None)
            else:
                nxt = scores(j + 1, 0, diag_slot(j + 1))
            accumulate(j, u, *pending)
            pending = nxt
        return pending

    pending = scores(0, 0, diag_slot(0))
    pending = lax.fori_loop(0, qi, body, pending)
    for u in range(n_units):
        nxt = scores(qi, u + 1, 1) if u + 1 < n_units else None
        accumulate(qi, u, *pending)
        pending = nxt

    lam = (jnp.exp(jnp.sum(lq1_ref[...] * lk1_ref[...], axis=-1, keepdims=True))
           - jnp.exp(jnp.sum(lq2_ref[...] * lk2_ref[...], axis=-1, keepdims=True))
           + lam_init)
    for h in heads:
        cols = slice(h * HEAD_W, (h + 1) * HEAD_W)
        acc = acc_ref[h]
        inv_l = 1.0 / acc[HEAD_W:HEAD_W + 1, :]
        o_t = (acc[0:HEAD_W, :tq] * inv_l[:, :tq]
               - acc[0:HEAD_W, tq:] * (lam * inv_l[:, tq:]))
        ms = jnp.mean(o_t * o_t, axis=0, keepdims=True)
        o_t = o_t * (lax.rsqrt(ms + LN_EPS) * (1.0 - lam_init))
        o = o_t.T * subln_ref[...]
        o_ref[:, cols] = (o * _silu(g_ref[:, cols])).astype(BF16)


def _diff_attn(qz, k, vt, g_attn, lam_q1, lam_k1, lam_q2, lam_k2, subln_g, *, lam_init):
    B, S, _ = k.shape
    tq, tk = SEQ_TILE, KEY_TILE
    assert tq == tk == QBLK, "one causal bias tile serves every unit of the diagonal tile"
    nt = S // tq
    kern = functools.partial(_diff_attn_kernel, tq=tq, tk=tk, lam_init=lam_init)
    vec = lambda n: pl.BlockSpec((1, n), lambda b, i: (0, 0))
    return pl.pallas_call(
        kern,
        grid=(B, nt),
        in_specs=[
            vec(HEAD_DIM), vec(HEAD_DIM), vec(HEAD_DIM), vec(HEAD_DIM), vec(HEAD_W),
            pl.BlockSpec((None, None, ATTN_W, 2 * tq), lambda b, i: (b, i, 0, 0)),
            pl.BlockSpec((None, S, ATTN_W), lambda b, i: (b, 0, 0)),
            pl.BlockSpec((None, S // tk, N_HEADS * V_ROWS, tk), lambda b, i: (b, 0, 0, 0)),
            pl.BlockSpec((None, tq, ATTN_W), lambda b, i: (b, i, 0)),
        ],
        out_specs=pl.BlockSpec((None, tq, ATTN_W), lambda b, i: (b, i, 0)),
        out_shape=jax.ShapeDtypeStruct((B, S, ATTN_W), BF16),
        scratch_shapes=[
            pltpu.VMEM((N_HEADS, 1, 2 * tq), F32),
            pltpu.VMEM((N_HEADS, V_ROWS, 2 * tq), F32),
            pltpu.VMEM((2, tk, QBLK), F32),
            pltpu.VMEM((2, tk, QBLK), F32),
        ],
        compiler_params=pltpu.CompilerParams(
            dimension_semantics=("arbitrary", "arbitrary"),
            vmem_limit_bytes=V7X_VMEM_LIMIT_BYTES),
        name="diff_attn",
    )(lam_q1, lam_k1, lam_q2, lam_k2, subln_g, qz, k, vt, g_attn)


def _out_proj_kernel(x_ref, attn_ref, pool_ref, p_ref, wout_ref, lng_ref, lnb_ref,
                     wpe_ref, wpg_ref, bpg_ref, o_ref, *, alpha):
    mix = (jnp.dot(attn_ref[...], wout_ref[0:ATTN_W, :], preferred_element_type=F32)
           + jnp.dot(pool_ref[...], wout_ref[ATTN_W:, :], preferred_element_type=F32))
    r = alpha * x_ref[...] + mix
    mu = jnp.mean(r, axis=-1, keepdims=True)
    c = r - mu
    var = jnp.mean(c * c, axis=-1, keepdims=True)
    h = c * lax.rsqrt(var + LN_EPS) * lng_ref[...] + lnb_ref[...]
    logits = jnp.dot(h.astype(BF16), wpg_ref[...], preferred_element_type=F32) + bpg_ref[...]
    gate = 1.0 / (1.0 + jnp.exp(-logits))
    pe = jnp.dot(p_ref[...].astype(BF16), wpe_ref[...], preferred_element_type=F32)
    o_ref[...] = h + pe * gate


def _out_proj(x2, attn2, pool2, p2, w_out_bf, ln_g, ln_b, w_pe_bf, w_pg_bf, b_pg, *, alpha):
    R = x2.shape[0]
    tm = SEQ_TILE
    kern = functools.partial(_out_proj_kernel, alpha=alpha)
    rows = lambda n: pl.BlockSpec((tm, n), lambda i: (i, 0))
    full = lambda a, b: pl.BlockSpec((a, b), lambda i: (0, 0))
    return pl.pallas_call(
        kern,
        grid=(R // tm,),
        in_specs=[
            rows(D_MODEL), rows(ATTN_W), rows(POOL_W), rows(PLE_DIM),
            full(ATTN_W + POOL_W, D_MODEL), full(1, D_MODEL), full(1, D_MODEL),
            full(PLE_DIM, D_MODEL), full(D_MODEL, D_MODEL), full(1, D_MODEL),
        ],
        out_specs=rows(D_MODEL),
        out_shape=jax.ShapeDtypeStruct((R, D_MODEL), F32),
        compiler_params=pltpu.CompilerParams(
            dimension_semantics=("arbitrary",),
            vmem_limit_bytes=V7X_VMEM_LIMIT_BYTES),
        name="out_proj_ln_ple",
    )(x2, attn2, pool2, p2, w_out_bf, ln_g, ln_b, w_pe_bf, w_pg_bf, b_pg)


def kernel(x, p, w_in, lam_q1, lam_k1, lam_q2, lam_k2, subln_g, w_pool, pool_scale,
           w_out, ln_g, ln_b, w_pe, w_pg, b_pg):
    B, S, D = x.shape
    depth = w_in.shape[0]
    h = x
    for i in range(depth):
        alpha = (2.0 * depth) ** 0.25
        w = w_in[i]
        wqv_t = jnp.concatenate([w[:, 0:ATTN_W], w[:, 2 * ATTN_W:3 * ATTN_W]], axis=1).T.astype(BF16)
        w_rest = jnp.concatenate([w[:, ATTN_W:2 * ATTN_W], w[:, 3 * ATTN_W:]], axis=1).astype(BF16)
        qz, k, vt, g_attn, pool_mix = _in_proj_pool(
            h, wqv_t, w_rest, w_pool[i].astype(BF16), pool_scale[i][None, :])
        attn = _diff_attn(
            qz, k, vt, g_attn, lam_q1[i][None, :], lam_k1[i][None, :], lam_q2[i][None, :],
            lam_k2[i][None, :], subln_g[i][None, :], lam_init=_lambda_init(i))
        out = _out_proj(
            h.reshape(B * S, D), attn.reshape(B * S, ATTN_W), pool_mix.reshape(B * S, POOL_W),
            p[i].reshape(B * S, PLE_DIM), w_out[i].astype(BF16), ln_g[i][None, :],
            ln_b[i][None, :], w_pe[i].astype(BF16), w_pg[i].astype(BF16), b_pg[i][None, :],
            alpha=alpha)
        h = out.reshape(B, S, D)
    return h
```

```python
import functools
import math

import jax
import jax.numpy as jnp
from jax import lax
from jax.experimental import pallas as pl
from jax.experimental.pallas import tpu as pltpu

D_MODEL = 1024
PLE_DIM = 256
N_HEADS = 4
HEAD_DIM = 64
HEAD_W = 2 * HEAD_DIM
ATTN_W = N_HEADS * HEAD_W
N_POOL_GROUPS = 4
POOL_WINDOWS = (2, 4, 8, 16)
POOL_CH = 128
POOL_W = N_POOL_GROUPS * POOL_CH
REST_W = ATTN_W + ATTN_W + 2 * POOL_W
LN_EPS = 1e-5
LOG2_E = math.log2(math.e)
POOL_HALO = max(POOL_WINDOWS)

BF16_SUBLANE_TILE = 16
V_ROWS = HEAD_W + BF16_SUBLANE_TILE

SEQ_TILE = 512
KEY_TILE = 512
Q_SUBTILES = 2
V7X_VMEM_LIMIT_BYTES = 56 * 1024 * 1024

BF16 = jnp.bfloat16
F32 = jnp.float32
NT_DIMS = (((1,), (1,)), ((), ()))


def _lambda_init(layer_idx):
    return 0.8 - 0.6 * math.exp(-0.3 * layer_idx)


def _silu(g):
    return g / (1.0 + jnp.exp(-g))


def _in_proj_pool_kernel(x_ref, wqv_ref, wrest_ref, wpool_ref, pscale_ref,
                         qz_ref, k_ref, vt_ref, gattn_ref, pool_ref, ubuf_ref, *, tm, tk):
    i = pl.program_id(1)
    xb = x_ref[...].astype(BF16)

    scale = LOG2_E / math.sqrt(HEAD_DIM)
    q_t = lax.dot_general(wqv_ref[0:ATTN_W, :], xb, NT_DIMS, preferred_element_type=F32)
    v_t = lax.dot_general(wqv_ref[ATTN_W:, :], xb, NT_DIMS, preferred_element_type=F32)
    zeros = jnp.zeros((HEAD_DIM, tm), BF16)
    ones_tile = jnp.where(
        lax.broadcasted_iota(jnp.int32, (BF16_SUBLANE_TILE, tk), 0) == 0, 1.0, 0.0).astype(BF16)
    for h in range(N_HEADS):
        f0 = h * HEAD_W
        qz_ref[f0:f0 + HEAD_DIM, 0:tm] = (q_t[f0:f0 + HEAD_DIM] * scale).astype(BF16)
        qz_ref[f0 + HEAD_DIM:f0 + HEAD_W, 0:tm] = zeros
        qz_ref[f0:f0 + HEAD_DIM, tm:] = zeros
        qz_ref[f0 + HEAD_DIM:f0 + HEAD_W, tm:] = (q_t[f0 + HEAD_DIM:f0 + HEAD_W] * scale).astype(BF16)
        r0 = h * V_ROWS
        for c in range(tm // tk):
            vt_ref[c, r0:r0 + HEAD_W, :] = v_t[f0:f0 + HEAD_W, c * tk:(c + 1) * tk].astype(BF16)
            vt_ref[c, r0 + HEAD_W:r0 + V_ROWS, :] = ones_tile

    def proj(lo, width):
        return jnp.dot(xb, wrest_ref[:, lo:lo + width], preferred_element_type=F32)

    k_ref[...] = proj(0, ATTN_W).astype(BF16)
    gattn_ref[...] = proj(ATTN_W, ATTN_W)

    @pl.when(i == 0)
    def _():
        ubuf_ref[0:POOL_HALO, :] = jnp.zeros((POOL_HALO, POOL_W), F32)

    @pl.when(i > 0)
    def _():
        ubuf_ref[0:POOL_HALO, :] = ubuf_ref[tm:tm + POOL_HALO, :]

    ubuf_ref[POOL_HALO:, :] = proj(2 * ATTN_W, POOL_W)
    g_pool = proj(2 * ATTN_W + POOL_W, POOL_W)

    t = (i * tm + lax.broadcasted_iota(jnp.int32, (tm, 1), 0) + 1).astype(F32)
    for g, w in enumerate(POOL_WINDOWS):
        cols = slice(g * POOL_CH, (g + 1) * POOL_CH)
        ext = ubuf_ref[:, cols]
        acc = ext
        span = 1
        while span < w:
            acc = acc + pltpu.roll(acc, span, axis=0)
            span *= 2
        u = ext[POOL_HALO:, :]
        mean = acc[POOL_HALO:, :] / jnp.minimum(t, float(w))
        d = (mean - u).astype(BF16)
        y = jnp.dot(d, wpool_ref[g], preferred_element_type=F32)
        y = y * pscale_ref[:, cols] * _silu(g_pool[:, cols])
        pool_ref[:, cols] = y.astype(BF16)


def _in_proj_pool(x, wqv_t, w_rest, w_pool_bf, pool_scale):
    B, S, _ = x.shape
    tm, tk = SEQ_TILE, KEY_TILE
    nt = S // tm
    kern = functools.partial(_in_proj_pool_kernel, tm=tm, tk=tk)
    return pl.pallas_call(
        kern,
        grid=(B, nt),
        in_specs=[
            pl.BlockSpec((None, tm, D_MODEL), lambda b, i: (b, i, 0)),
            pl.BlockSpec((2 * ATTN_W, D_MODEL), lambda b, i: (0, 0)),
            pl.BlockSpec((D_MODEL, REST_W), lambda b, i: (0, 0)),
            pl.BlockSpec((N_POOL_GROUPS, POOL_CH, POOL_CH), lambda b, i: (0, 0, 0)),
            pl.BlockSpec((1, POOL_W), lambda b, i: (0, 0)),
        ],
        out_specs=[
            pl.BlockSpec((None, None, ATTN_W, 2 * tm), lambda b, i: (b, i, 0, 0)),
            pl.BlockSpec((None, tm, ATTN_W), lambda b, i: (b, i, 0)),
            pl.BlockSpec((None, tm // tk, N_HEADS * V_ROWS, tk), lambda b, i: (b, i, 0, 0)),
            pl.BlockSpec((None, tm, ATTN_W), lambda b, i: (b, i, 0)),
            pl.BlockSpec((None, tm, POOL_W), lambda b, i: (b, i, 0)),
        ],
        out_shape=[
            jax.ShapeDtypeStruct((B, nt, ATTN_W, 2 * tm), BF16),
            jax.ShapeDtypeStruct((B, S, ATTN_W), BF16),
            jax.ShapeDtypeStruct((B, S // tk, N_HEADS * V_ROWS, tk), BF16),
            jax.ShapeDtypeStruct((B, S, ATTN_W), F32),
            jax.ShapeDtypeStruct((B, S, POOL_W), BF16),
        ],
        scratch_shapes=[pltpu.VMEM((POOL_HALO + tm, POOL_W), F32)],
        compiler_params=pltpu.CompilerParams(
            dimension_semantics=("arbitrary", "arbitrary"),
            vmem_limit_bytes=V7X_VMEM_LIMIT_BYTES),
        name="in_proj_pool",
    )(x, wqv_t, w_rest, w_pool_bf, pool_scale)


def _diff_attn_kernel(lq1_ref, lk1_ref, lq2_ref, lk2_ref, subln_ref,
                      qz_ref, k_ref, vt_ref, g_ref, o_ref,
                      m_ref, acc_ref, s_ref, bias_ref, *, qt, tq, tk, lam_init):
    qi = pl.program_id(1)
    first_diag = qi * qt

    @pl.when((pl.program_id(0) == 0) & (qi == 0))
    def _():
        key = lax.broadcasted_iota(jnp.int32, (tk, tq), 0)
        qry = lax.broadcasted_iota(jnp.int32, (tk, tq), 1)
        bias_ref[0] = jnp.zeros((tk, tq), F32)
        bias_ref[1] = jnp.where(key <= qry, 0.0, -jnp.inf).astype(F32)

    m_ref[...] = jnp.full(m_ref.shape, -jnp.inf, F32)
    acc_ref[...] = jnp.zeros(acc_ref.shape, F32)

    def scores(j, unit, bias_slot, slot):
        sub, h, n = unit
        qcols = slice(n * tq, (n + 1) * tq)
        k_start = pl.multiple_of(j * tk, tk)
        kj = k_ref[pl.ds(k_start, tk), h * HEAD_W:(h + 1) * HEAD_W]
        s_t = jnp.dot(kj, qz_ref[sub, h * HEAD_W:(h + 1) * HEAD_W, qcols],
                      preferred_element_type=F32)
        if bias_slot is not None:
            s_t = s_t + bias_ref[bias_slot]
        m_prev = m_ref[sub, h, :, qcols]
        m_new = jnp.maximum(m_prev, jnp.max(s_t, axis=0, keepdims=True))
        m_ref[sub, h, :, qcols] = m_new
        s_ref[slot] = s_t
        return m_prev, m_new

    def accumulate(j, unit, slot, m_prev, m_new):
        sub, h, n = unit
        qcols = slice(n * tq, (n + 1) * tq)
        vj = vt_ref[j, h * V_ROWS:(h + 1) * V_ROWS, :]
        alpha = jnp.exp2(m_prev - m_new)
        p_t = jnp.exp2(s_ref[slot] - m_new).astype(BF16)
        acc_ref[sub, h, :, qcols] = (alpha * acc_ref[sub, h, :, qcols]
                                     + jnp.dot(vj, p_t, preferred_element_type=F32))

    head_units = [(h, n) for h in range(N_HEADS) for n in range(2)]
    full_units = [(sub, h, n) for sub in range(qt) for (h, n) in head_units]
    first_unit = full_units[0]
    assert len(full_units) % 2 == 0

    def first_unit_bias_slot(j):
        return (j == first_diag).astype(jnp.int32)

    def body(j, pending):
        for idx, unit in enumerate(full_units):
            if idx + 1 < len(full_units):
                nxt = scores(j, full_units[idx + 1], None, (idx + 1) % 2)
            else:
                nxt = scores(j + 1, first_unit, first_unit_bias_slot(j + 1), 0)
            accumulate(j, unit, idx % 2, *pending)
            pending = nxt
        return pending

    pending = scores(0, first_unit, first_unit_bias_slot(0), 0)
    pending = lax.fori_loop(0, first_diag, body, pending)

    diag_items = [(d, (sub, h, n), 1 if sub == d else None)
                  for d in range(qt) for sub in range(d, qt) for (h, n) in head_units]
    for idx, (d, unit, bias_slot) in enumerate(diag_items):
        if idx + 1 < len(diag_items):
            d2, unit2, bias2 = diag_items[idx + 1]
            nxt = scores(first_diag + d2, unit2, bias2, (idx + 1) % 2)
        else:
            nxt = None
        accumulate(first_diag + d, unit, idx % 2, *pending)
        pending = nxt

    lam = (jnp.exp(jnp.sum(lq1_ref[...] * lk1_ref[...], axis=-1, keepdims=True))
           - jnp.exp(jnp.sum(lq2_ref[...] * lk2_ref[...], axis=-1, keepdims=True))
           + lam_init)
    for sub in range(qt):
        rows = slice(sub * tq, (sub + 1) * tq)
        for h in range(N_HEADS):
            cols = slice(h * HEAD_W, (h + 1) * HEAD_W)
            acc = acc_ref[sub, h]
            inv_l = 1.0 / acc[HEAD_W:HEAD_W + 1, :]
            o_t = (acc[0:HEAD_W, :tq] * inv_l[:, :tq]
                   - acc[0:HEAD_W, tq:] * (lam * inv_l[:, tq:]))
            ms = jnp.mean(o_t * o_t, axis=0, keepdims=True)
            o_t = o_t * (lax.rsqrt(ms + LN_EPS) * (1.0 - lam_init))
            o = o_t.T * subln_ref[...]
            o_ref[rows, cols] = (o * _silu(g_ref[rows, cols])).astype(BF16)


def _diff_attn(qz, k, vt, g_attn, lam_q1, lam_k1, lam_q2, lam_k2, subln_g, *, lam_init):
    B, S, _ = k.shape
    tq, tk, qt = SEQ_TILE, KEY_TILE, Q_SUBTILES
    assert tq == tk, "a diagonal block must be square for the shared causal bias tile"
    nt = S // tq
    kern = functools.partial(_diff_attn_kernel, qt=qt, tq=tq, tk=tk, lam_init=lam_init)
    vec = lambda n: pl.BlockSpec((1, n), lambda b, i: (0, 0))
    return pl.pallas_call(
        kern,
        grid=(B, nt // qt),
        in_specs=[
            vec(HEAD_DIM), vec(HEAD_DIM), vec(HEAD_DIM), vec(HEAD_DIM), vec(HEAD_W),
            pl.BlockSpec((None, qt, ATTN_W, 2 * tq), lambda b, i: (b, i, 0, 0)),
            pl.BlockSpec((None, S, ATTN_W), lambda b, i: (b, 0, 0)),
            pl.BlockSpec((None, S // tk, N_HEADS * V_ROWS, tk), lambda b, i: (b, 0, 0, 0)),
            pl.BlockSpec((None, qt * tq, ATTN_W), lambda b, i: (b, i, 0)),
        ],
        out_specs=pl.BlockSpec((None, qt * tq, ATTN_W), lambda b, i: (b, i, 0)),
        out_shape=jax.ShapeDtypeStruct((B, S, ATTN_W), BF16),
        scratch_shapes=[
            pltpu.VMEM((qt, N_HEADS, 1, 2 * tq), F32),
            pltpu.VMEM((qt, N_HEADS, V_ROWS, 2 * tq), F32),
            pltpu.VMEM((2, tk, tq), F32),
            pltpu.VMEM((2, tk, tq), F32),
        ],
        compiler_params=pltpu.CompilerParams(
            dimension_semantics=("arbitrary", "arbitrary"),
            vmem_limit_bytes=V7X_VMEM_LIMIT_BYTES),
        name="diff_attn",
    )(lam_q1, lam_k1, lam_q2, lam_k2, subln_g, qz, k, vt, g_attn)


def _out_proj_kernel(x_ref, attn_ref, pool_ref, p_ref, wout_ref, lng_ref, lnb_ref,
                     wpe_ref, wpg_ref, bpg_ref, o_ref, *, alpha):
    mix = (jnp.dot(attn_ref[...], wout_ref[0:ATTN_W, :], preferred_element_type=F32)
           + jnp.dot(pool_ref[...], wout_ref[ATTN_W:, :], preferred_element_type=F32))
    r = alpha * x_ref[...] + mix
    mu = jnp.mean(r, axis=-1, keepdims=True)
    c = r - mu
    var = jnp.mean(c * c, axis=-1, keepdims=True)
    h = c * lax.rsqrt(var + LN_EPS) * lng_ref[...] + lnb_ref[...]
    logits = jnp.dot(h.astype(BF16), wpg_ref[...], preferred_element_type=F32) + bpg_ref[...]
    gate = 1.0 / (1.0 + jnp.exp(-logits))
    pe = jnp.dot(p_ref[...].astype(BF16), wpe_ref[...], preferred_element_type=F32)
    o_ref[...] = h + pe * gate


def _out_proj(x2, attn2, pool2, p2, w_out_bf, ln_g, ln_b, w_pe_bf, w_pg_bf, b_pg, *, alpha):
    R = x2.shape[0]
    tm = SEQ_TILE
    kern = functools.partial(_out_proj_kernel, alpha=alpha)
    rows = lambda n: pl.BlockSpec((tm, n), lambda i: (i, 0))
    full = lambda a, b: pl.BlockSpec((a, b), lambda i: (0, 0))
    return pl.pallas_call(
        kern,
        grid=(R // tm,),
        in_specs=[
            rows(D_MODEL), rows(ATTN_W), rows(POOL_W), rows(PLE_DIM),
            full(ATTN_W + POOL_W, D_MODEL), full(1, D_MODEL), full(1, D_MODEL),
            full(PLE_DIM, D_MODEL), full(D_MODEL, D_MODEL), full(1, D_MODEL),
        ],
        out_specs=rows(D_MODEL),
        out_shape=jax.ShapeDtypeStruct((R, D_MODEL), F32),
        compiler_params=pltpu.CompilerParams(
            dimension_semantics=("arbitrary",),
            vmem_limit_bytes=V7X_VMEM_LIMIT_BYTES),
        name="out_proj_ln_ple",
    )(x2, attn2, pool2, p2, w_out_bf, ln_g, ln_b, w_pe_bf, w_pg_bf, b_pg)


def kernel(x, p, w_in, lam_q1, lam_k1, lam_q2, lam_k2, subln_g, w_pool, pool_scale,
           w_out, ln_g, ln_b, w_pe, w_pg, b_pg):
    B, S, D = x.shape
    depth = w_in.shape[0]
    h = x
    for i in range(depth):
        alpha = (2.0 * depth) ** 0.25
        w = w_in[i]
        wqv_t = jnp.concatenate([w[:, 0:ATTN_W], w[:, 2 * ATTN_W:3 * ATTN_W]], axis=1).T.astype(BF16)
        w_rest = jnp.concatenate([w[:, ATTN_W:2 * ATTN_W], w[:, 3 * ATTN_W:]], axis=1).astype(BF16)
        qz, k, vt, g_attn, pool_mix = _in_proj_pool(
            h, wqv_t, w_rest, w_pool[i].astype(BF16), pool_scale[i][None, :])
        attn = _diff_attn(
            qz, k, vt, g_attn, lam_q1[i][None, :], lam_k1[i][None, :], lam_q2[i][None, :],
            lam_k2[i][None, :], subln_g[i][None, :], lam_init=_lambda_init(i))
        out = _out_proj(
            h.reshape(B * S, D), attn.reshape(B * S, ATTN_W), pool_mix.reshape(B * S, POOL_W),
            p[i].reshape(B * S, PLE_DIM), w_out[i].astype(BF16), ln_g[i][None, :],
            ln_b[i][None, :], w_pe[i].astype(BF16), w_pg[i].astype(BF16), b_pg[i][None, :],
            alpha=alpha)
        h = out.reshape(B, S, D)
    return h
```

```python
import functools
import math

import jax
import jax.numpy as jnp
from jax import lax
from jax.experimental import pallas as pl
from jax.experimental.pallas import tpu as pltpu

D_MODEL = 1024
PLE_DIM = 256
N_HEADS = 4
HEAD_DIM = 64
HEAD_W = 2 * HEAD_DIM
ATTN_W = N_HEADS * HEAD_W
N_POOL_GROUPS = 4
POOL_WINDOWS = (2, 4, 8, 16)
POOL_CH = 128
POOL_W = N_POOL_GROUPS * POOL_CH
REST_W = ATTN_W + ATTN_W + 2 * POOL_W
LN_EPS = 1e-5
LOG2_E = math.log2(math.e)
POOL_HALO = max(POOL_WINDOWS)

BF16_SUBLANE_TILE = 16
V_ROWS = HEAD_W + BF16_SUBLANE_TILE

SEQ_TILE = 512
KEY_TILE = 512
Q_SUBTILES = 2
OUT_TILE = 1024
OUT_CHUNK = 256
V7X_VMEM_LIMIT_BYTES = 56 * 1024 * 1024

BF16 = jnp.bfloat16
F32 = jnp.float32
NT_DIMS = (((1,), (1,)), ((), ()))


def _lambda_init(layer_idx):
    return 0.8 - 0.6 * math.exp(-0.3 * layer_idx)


def _silu(g):
    return g / (1.0 + jnp.exp(-g))


def _in_proj_pool_kernel(x_ref, wqv_ref, wrest_ref, wpool_ref, pscale_ref,
                         qz_ref, k_ref, vt_ref, gattn_ref, pool_ref, ubuf_ref, *, tm, tk):
    i = pl.program_id(1)

    @pl.when(i == 0)
    def _():
        ubuf_ref[0:POOL_HALO, :] = jnp.zeros((POOL_HALO, POOL_W), F32)

    @pl.when(i > 0)
    def _():
        ubuf_ref[0:POOL_HALO, :] = ubuf_ref[tm:tm + POOL_HALO, :]

    xb = x_ref[...].astype(BF16)

    def proj(lo, width):
        return jnp.dot(xb, wrest_ref[:, lo:lo + width], preferred_element_type=F32)

    ubuf_ref[POOL_HALO:, :] = proj(2 * ATTN_W, POOL_W)
    pool_gate = pscale_ref[...] * _silu(proj(2 * ATTN_W + POOL_W, POOL_W))

    t = (i * tm + lax.broadcasted_iota(jnp.int32, (tm, 1), 0) + 1).astype(F32)
    deltas = []
    for g, w in enumerate(POOL_WINDOWS):
        ext = ubuf_ref[:, g * POOL_CH:(g + 1) * POOL_CH]
        acc = ext
        span = 1
        while span < w:
            acc = acc + pltpu.roll(acc, span, axis=0)
            span *= 2
        mean = acc[POOL_HALO:, :] / jnp.minimum(t, float(w))
        deltas.append((mean - ext[POOL_HALO:, :]).astype(BF16))

    scale = LOG2_E / math.sqrt(HEAD_DIM)
    q_t = lax.dot_general(wqv_ref[0:ATTN_W, :], xb, NT_DIMS, preferred_element_type=F32)
    v_t = lax.dot_general(wqv_ref[ATTN_W:, :], xb, NT_DIMS, preferred_element_type=F32)
    zeros = jnp.zeros((HEAD_DIM, tm), BF16)
    ones_tile = jnp.where(
        lax.broadcasted_iota(jnp.int32, (BF16_SUBLANE_TILE, tk), 0) == 0, 1.0, 0.0).astype(BF16)
    for h in range(N_HEADS):
        f0 = h * HEAD_W
        qz_ref[f0:f0 + HEAD_DIM, 0:tm] = (q_t[f0:f0 + HEAD_DIM] * scale).astype(BF16)
        qz_ref[f0 + HEAD_DIM:f0 + HEAD_W, 0:tm] = zeros
        qz_ref[f0:f0 + HEAD_DIM, tm:] = zeros
        qz_ref[f0 + HEAD_DIM:f0 + HEAD_W, tm:] = (q_t[f0 + HEAD_DIM:f0 + HEAD_W] * scale).astype(BF16)
        r0 = h * V_ROWS
        for c in range(tm // tk):
            vt_ref[c, r0:r0 + HEAD_W, :] = v_t[f0:f0 + HEAD_W, c * tk:(c + 1) * tk].astype(BF16)
            vt_ref[c, r0 + HEAD_W:r0 + V_ROWS, :] = ones_tile

    k_ref[...] = proj(0, ATTN_W).astype(BF16)
    gattn_ref[...] = proj(ATTN_W, ATTN_W)

    for g in range(N_POOL_GROUPS):
        cols = slice(g * POOL_CH, (g + 1) * POOL_CH)
        y = jnp.dot(deltas[g], wpool_ref[g], preferred_element_type=F32)
        pool_ref[:, cols] = (y * pool_gate[:, cols]).astype(BF16)


def _in_proj_pool(x, wqv_t, w_rest, w_pool_bf, pool_scale):
    B, S, _ = x.shape
    tm, tk = SEQ_TILE, KEY_TILE
    nt = S // tm
    kern = functools.partial(_in_proj_pool_kernel, tm=tm, tk=tk)
    return pl.pallas_call(
        kern,
        grid=(B, nt),
        in_specs=[
            pl.BlockSpec((None, tm, D_MODEL), lambda b, i: (b, i, 0)),
            pl.BlockSpec((2 * ATTN_W, D_MODEL), lambda b, i: (0, 0)),
            pl.BlockSpec((D_MODEL, REST_W), lambda b, i: (0, 0)),
            pl.BlockSpec((N_POOL_GROUPS, POOL_CH, POOL_CH), lambda b, i: (0, 0, 0)),
            pl.BlockSpec((1, POOL_W), lambda b, i: (0, 0)),
        ],
        out_specs=[
            pl.BlockSpec((None, None, ATTN_W, 2 * tm), lambda b, i: (b, i, 0, 0)),
            pl.BlockSpec((None, tm, ATTN_W), lambda b, i: (b, i, 0)),
            pl.BlockSpec((None, tm // tk, N_HEADS * V_ROWS, tk), lambda b, i: (b, i, 0, 0)),
            pl.BlockSpec((None, tm, ATTN_W), lambda b, i: (b, i, 0)),
            pl.BlockSpec((None, tm, POOL_W), lambda b, i: (b, i, 0)),
        ],
        out_shape=[
            jax.ShapeDtypeStruct((B, nt, ATTN_W, 2 * tm), BF16),
            jax.ShapeDtypeStruct((B, S, ATTN_W), BF16),
            jax.ShapeDtypeStruct((B, S // tk, N_HEADS * V_ROWS, tk), BF16),
            jax.ShapeDtypeStruct((B, S, ATTN_W), F32),
            jax.ShapeDtypeStruct((B, S, POOL_W), BF16),
        ],
        scratch_shapes=[pltpu.VMEM((POOL_HALO + tm, POOL_W), F32)],
        compiler_params=pltpu.CompilerParams(
            dimension_semantics=("arbitrary", "arbitrary"),
            vmem_limit_bytes=V7X_VMEM_LIMIT_BYTES),
        name="in_proj_pool",
    )(x, wqv_t, w_rest, w_pool_bf, pool_scale)


def _diff_attn_kernel(lq1_ref, lk1_ref, lq2_ref, lk2_ref, subln_ref,
                      qz_ref, k_ref, vt_ref, g_ref, o_ref,
                      m_ref, acc_ref, s_ref, bias_ref, *, qt, tq, tk, lam_init):
    qi = pl.program_id(1)
    first_diag = qi * qt

    @pl.when((pl.program_id(0) == 0) & (qi == 0))
    def _():
        key = lax.broadcasted_iota(jnp.int32, (tk, tq), 0)
        qry = lax.broadcasted_iota(jnp.int32, (tk, tq), 1)
        bias_ref[0] = jnp.zeros((tk, tq), F32)
        bias_ref[1] = jnp.where(key <= qry, 0.0, -jnp.inf).astype(F32)

    m_ref[...] = jnp.full(m_ref.shape, -jnp.inf, F32)
    acc_ref[...] = jnp.zeros(acc_ref.shape, F32)

    def scores(j, unit, bias_slot, slot):
        sub, h, n = unit
        qcols = slice(n * tq, (n + 1) * tq)
        k_start = pl.multiple_of(j * tk, tk)
        kj = k_ref[pl.ds(k_start, tk), h * HEAD_W:(h + 1) * HEAD_W]
        s_t = jnp.dot(kj, qz_ref[sub, h * HEAD_W:(h + 1) * HEAD_W, qcols],
                      preferred_element_type=F32)
        if bias_slot is not None:
            s_t = s_t + bias_ref[bias_slot]
        m_prev = m_ref[sub, h, :, qcols]
        m_new = jnp.maximum(m_prev, jnp.max(s_t, axis=0, keepdims=True))
        m_ref[sub, h, :, qcols] = m_new
        s_ref[slot] = s_t
        return m_prev, m_new

    def accumulate(j, unit, slot, m_prev, m_new):
        sub, h, n = unit
        qcols = slice(n * tq, (n + 1) * tq)
        vj = vt_ref[j, h * V_ROWS:(h + 1) * V_ROWS, :]
        alpha = jnp.exp2(m_prev - m_new)
        p_t = jnp.exp2(s_ref[slot] - m_new).astype(BF16)
        acc_ref[sub, h, :, qcols] = (alpha * acc_ref[sub, h, :, qcols]
                                     + jnp.dot(vj, p_t, preferred_element_type=F32))

    head_units = [(h, n) for h in range(N_HEADS) for n in range(2)]
    full_units = [(sub, h, n) for sub in range(qt) for (h, n) in head_units]
    first_unit = full_units[0]
    assert len(full_units) % 2 == 0

    def first_unit_bias_slot(j):
        return (j == first_diag).astype(jnp.int32)

    def body(j, pending):
        for idx, unit in enumerate(full_units):
            if idx + 1 < len(full_units):
                nxt = scores(j, full_units[idx + 1], None, (idx + 1) % 2)
            else:
                nxt = scores(j + 1, first_unit, first_unit_bias_slot(j + 1), 0)
            accumulate(j, unit, idx % 2, *pending)
            pending = nxt
        return pending

    pending = scores(0, first_unit, first_unit_bias_slot(0), 0)
    pending = lax.fori_loop(0, first_diag, body, pending)

    diag_items = [(d, (sub, h, n), 1 if sub == d else None)
                  for d in range(qt) for sub in range(d, qt) for (h, n) in head_units]
    for idx, (d, unit, bias_slot) in enumerate(diag_items):
        if idx + 1 < len(diag_items):
            d2, unit2, bias2 = diag_items[idx + 1]
            nxt = scores(first_diag + d2, unit2, bias2, (idx + 1) % 2)
        else:
            nxt = None
        accumulate(first_diag + d, unit, idx % 2, *pending)
        pending = nxt

    lam = (jnp.exp(jnp.sum(lq1_ref[...] * lk1_ref[...], axis=-1, keepdims=True))
           - jnp.exp(jnp.sum(lq2_ref[...] * lk2_ref[...], axis=-1, keepdims=True))
           + lam_init)
    for sub in range(qt):
        rows = slice(sub * tq, (sub + 1) * tq)
        for h in range(N_HEADS):
            cols = slice(h * HEAD_W, (h + 1) * HEAD_W)
            acc = acc_ref[sub, h]
            inv_l = 1.0 / acc[HEAD_W:HEAD_W + 1, :]
            o_t = (acc[0:HEAD_W, :tq] * inv_l[:, :tq]
                   - acc[0:HEAD_W, tq:] * (lam * inv_l[:, tq:]))
            ms = jnp.mean(o_t * o_t, axis=0, keepdims=True)
            o_t = o_t * (lax.rsqrt(ms + LN_EPS) * (1.0 - lam_init))
            o = o_t.T * subln_ref[...]
            o_ref[rows, cols] = (o * _silu(g_ref[rows, cols])).astype(BF16)


def _diff_attn(qz, k, vt, g_attn, lam_q1, lam_k1, lam_q2, lam_k2, subln_g, *, lam_init):
    B, S, _ = k.shape
    tq, tk, qt = SEQ_TILE, KEY_TILE, Q_SUBTILES
    assert tq == tk, "a diagonal block must be square for the shared causal bias tile"
    nt = S // tq
    kern = functools.partial(_diff_attn_kernel, qt=qt, tq=tq, tk=tk, lam_init=lam_init)
    vec = lambda n: pl.BlockSpec((1, n), lambda b, i: (0, 0))
    return pl.pallas_call(
        kern,
        grid=(B, nt // qt),
        in_specs=[
            vec(HEAD_DIM), vec(HEAD_DIM), vec(HEAD_DIM), vec(HEAD_DIM), vec(HEAD_W),
            pl.BlockSpec((None, qt, ATTN_W, 2 * tq), lambda b, i: (b, i, 0, 0)),
            pl.BlockSpec((None, S, ATTN_W), lambda b, i: (b, 0, 0)),
            pl.BlockSpec((None, S // tk, N_HEADS * V_ROWS, tk), lambda b, i: (b, 0, 0, 0)),
            pl.BlockSpec((None, qt * tq, ATTN_W), lambda b, i: (b, i, 0)),
        ],
        out_specs=pl.BlockSpec((None, qt * tq, ATTN_W), lambda b, i: (b, i, 0)),
        out_shape=jax.ShapeDtypeStruct((B, S, ATTN_W), BF16),
        scratch_shapes=[
            pltpu.VMEM((qt, N_HEADS, 1, 2 * tq), F32),
            pltpu.VMEM((qt, N_HEADS, V_ROWS, 2 * tq), F32),
            pltpu.VMEM((2, tk, tq), F32),
            pltpu.VMEM((2, tk, tq), F32),
        ],
        compiler_params=pltpu.CompilerParams(
            dimension_semantics=("arbitrary", "arbitrary"),
            vmem_limit_bytes=V7X_VMEM_LIMIT_BYTES),
        name="diff_attn",
    )(lam_q1, lam_k1, lam_q2, lam_k2, subln_g, qz, k, vt, g_attn)


def _out_proj_kernel(x_ref, attn_ref, pool_ref, p_ref, wout_ref, lng_ref, lnb_ref,
                     wpe_ref, wpg_ref, bpg_ref, o_ref, *, alpha, tm, chunk):
    def branch_matmuls(c):
        rows = slice(c * chunk, (c + 1) * chunk)
        mix = (jnp.dot(attn_ref[rows, :], wout_ref[0:ATTN_W, :], preferred_element_type=F32)
               + jnp.dot(pool_ref[rows, :], wout_ref[ATTN_W:, :], preferred_element_type=F32))
        pe = jnp.dot(p_ref[rows, :].astype(BF16), wpe_ref[...], preferred_element_type=F32)
        return mix, pe

    def norm_and_gate(c, mix, pe):
        rows = slice(c * chunk, (c + 1) * chunk)
        r = alpha * x_ref[rows, :] + mix
        mu = jnp.mean(r, axis=-1, keepdims=True)
        cen = r - mu
        var = jnp.mean(cen * cen, axis=-1, keepdims=True)
        h = cen * lax.rsqrt(var + LN_EPS) * lng_ref[...] + lnb_ref[...]
        logits = jnp.dot(h.astype(BF16), wpg_ref[...], preferred_element_type=F32) + bpg_ref[...]
        gate = 1.0 / (1.0 + jnp.exp(-logits))
        o_ref[rows, :] = h + pe * gate

    pending = branch_matmuls(0)
    for c in range(tm // chunk):
        nxt = branch_matmuls(c + 1) if (c + 1) * chunk < tm else None
        norm_and_gate(c, *pending)
        pending = nxt


def _out_proj(x2, attn2, pool2, p2, w_out_bf, ln_g, ln_b, w_pe_bf, w_pg_bf, b_pg, *, alpha):
    R = x2.shape[0]
    tm, chunk = OUT_TILE, OUT_CHUNK
    kern = functools.partial(_out_proj_kernel, alpha=alpha, tm=tm, chunk=chunk)
    rows = lambda n: pl.BlockSpec((tm, n), lambda i: (i, 0))
    full = lambda a, b: pl.BlockSpec((a, b), lambda i: (0, 0))
    return pl.pallas_call(
        kern,
        grid=(R // tm,),
        in_specs=[
            rows(D_MODEL), rows(ATTN_W), rows(POOL_W), rows(PLE_DIM),
            full(ATTN_W + POOL_W, D_MODEL), full(1, D_MODEL), full(1, D_MODEL),
            full(PLE_DIM, D_MODEL), full(D_MODEL, D_MODEL), full(1, D_MODEL),
        ],
        out_specs=rows(D_MODEL),
        out_shape=jax.ShapeDtypeStruct((R, D_MODEL), F32),
        compiler_params=pltpu.CompilerParams(
            dimension_semantics=("arbitrary",),
            vmem_limit_bytes=V7X_VMEM_LIMIT_BYTES),
        name="out_proj_ln_ple",
    )(x2, attn2, pool2, p2, w_out_bf, ln_g, ln_b, w_pe_bf, w_pg_bf, b_pg)


def kernel(x, p, w_in, lam_q1, lam_k1, lam_q2, lam_k2, subln_g, w_pool, pool_scale,
           w_out, ln_g, ln_b, w_pe, w_pg, b_pg):
    B, S, D = x.shape
    depth = w_in.shape[0]
    h = x
    for i in range(depth):
        alpha = (2.0 * depth) ** 0.25
        w = w_in[i]
        wqv_t = jnp.concatenate([w[:, 0:ATTN_W], w[:, 2 * ATTN_W:3 * ATTN_W]], axis=1).T.astype(BF16)
        w_rest = jnp.concatenate([w[:, ATTN_W:2 * ATTN_W], w[:, 3 * ATTN_W:]], axis=1).astype(BF16)
        qz, k, vt, g_attn, pool_mix = _in_proj_pool(
            h, wqv_t, w_rest, w_pool[i].astype(BF16), pool_scale[i][None, :])
        attn = _diff_attn(
            qz, k, vt, g_attn, lam_q1[i][None, :], lam_k1[i][None, :], lam_q2[i][None, :],
            lam_k2[i][None, :], subln_g[i][None, :], lam_init=_lambda_init(i))
        out = _out_proj(
            h.reshape(B * S, D), attn.reshape(B * S, ATTN_W), pool_mix.reshape(B * S, POOL_W),
            p[i].reshape(B * S, PLE_DIM), w_out[i].astype(BF16), ln_g[i][None, :],
            ln_b[i][None, :], w_pe[i].astype(BF16), w_pg[i].astype(BF16), b_pg[i][None, :],
            alpha=alpha)
        h = out.reshape(B, S, D)
    return h
```

```python
import functools
import math

import jax
import jax.numpy as jnp
from jax import lax
from jax.experimental import pallas as pl
from jax.experimental.pallas import tpu as pltpu

D_MODEL = 1024
PLE_DIM = 256
N_HEADS = 4
HEAD_DIM = 64
HEAD_W = 2 * HEAD_DIM
ATTN_W = N_HEADS * HEAD_W
N_POOL_GROUPS = 4
POOL_WINDOWS = (2, 4, 8, 16)
POOL_CH = 128
POOL_W = N_POOL_GROUPS * POOL_CH
REST_W = ATTN_W + ATTN_W + 2 * POOL_W
LN_EPS = 1e-5
LOG2_E = math.log2(math.e)
POOL_HALO = max(POOL_WINDOWS)

BF16_SUBLANE_TILE = 16
V_ROWS = HEAD_W + BF16_SUBLANE_TILE

SEQ_TILE = 512
KEY_TILE = 512
Q_SUBTILES = 2
OUT_TILE = 1024
OUT_CHUNK = 256
V7X_VMEM_LIMIT_BYTES = 56 * 1024 * 1024

BF16 = jnp.bfloat16
F32 = jnp.float32
NT_DIMS = (((1,), (1,)), ((), ()))


def _lambda_init(layer_idx):
    return 0.8 - 0.6 * math.exp(-0.3 * layer_idx)


def _silu(g):
    return g / (1.0 + jnp.exp(-g))


def _in_proj_pool_kernel(x_ref, wqv_ref, wrest_ref, wpool_ref, pscale_ref,
                         qz_ref, k_ref, vt_ref, gattn_ref, pool_ref, ubuf_ref, *, tm, tk):
    i = pl.program_id(1)

    @pl.when(i == 0)
    def _():
        ubuf_ref[0:POOL_HALO, :] = jnp.zeros((POOL_HALO, POOL_W), F32)

    @pl.when(i > 0)
    def _():
        ubuf_ref[0:POOL_HALO, :] = ubuf_ref[tm:tm + POOL_HALO, :]

    xb = x_ref[...].astype(BF16)

    def proj(lo, width):
        return jnp.dot(xb, wrest_ref[:, lo:lo + width], preferred_element_type=F32)

    ubuf_ref[POOL_HALO:, :] = proj(2 * ATTN_W, POOL_W)
    pool_gate = pscale_ref[...] * _silu(proj(2 * ATTN_W + POOL_W, POOL_W))

    t = (i * tm + lax.broadcasted_iota(jnp.int32, (tm, 1), 0) + 1).astype(F32)
    deltas = []
    for g, w in enumerate(POOL_WINDOWS):
        ext = ubuf_ref[:, g * POOL_CH:(g + 1) * POOL_CH]
        acc = ext
        span = 1
        while span < w:
            acc = acc + pltpu.roll(acc, span, axis=0)
            span *= 2
        mean = acc[POOL_HALO:, :] / jnp.minimum(t, float(w))
        deltas.append((mean - ext[POOL_HALO:, :]).astype(BF16))

    scale = LOG2_E / math.sqrt(HEAD_DIM)
    q_t = lax.dot_general(wqv_ref[0:ATTN_W, :], xb, NT_DIMS, preferred_element_type=F32)
    v_t = lax.dot_general(wqv_ref[ATTN_W:, :], xb, NT_DIMS, preferred_element_type=F32)
    hq = tm // 2
    zeros = jnp.zeros((HEAD_DIM, hq), BF16)
    ones_tile = jnp.where(
        lax.broadcasted_iota(jnp.int32, (BF16_SUBLANE_TILE, tk), 0) == 0, 1.0, 0.0).astype(BF16)
    for h in range(N_HEADS):
        f0 = h * HEAD_W
        q1 = (q_t[f0:f0 + HEAD_DIM] * scale).astype(BF16)
        q2 = (q_t[f0 + HEAD_DIM:f0 + HEAD_W] * scale).astype(BF16)
        for half in range(2):
            qs = slice(half * hq, (half + 1) * hq)
            c1 = slice(half * tm, half * tm + hq)
            c2 = slice(half * tm + hq, (half + 1) * tm)
            qz_ref[f0:f0 + HEAD_DIM, c1] = q1[:, qs]
            qz_ref[f0 + HEAD_DIM:f0 + HEAD_W, c1] = zeros
            qz_ref[f0:f0 + HEAD_DIM, c2] = zeros
            qz_ref[f0 + HEAD_DIM:f0 + HEAD_W, c2] = q2[:, qs]
        r0 = h * V_ROWS
        for c in range(tm // tk):
            vt_ref[c, r0:r0 + HEAD_W, :] = v_t[f0:f0 + HEAD_W, c * tk:(c + 1) * tk].astype(BF16)
            vt_ref[c, r0 + HEAD_W:r0 + V_ROWS, :] = ones_tile

    k_ref[...] = proj(0, ATTN_W).astype(BF16)
    gattn_ref[...] = proj(ATTN_W, ATTN_W)

    for g in range(N_POOL_GROUPS):
        cols = slice(g * POOL_CH, (g + 1) * POOL_CH)
        y = jnp.dot(deltas[g], wpool_ref[g], preferred_element_type=F32)
        pool_ref[:, cols] = (y * pool_gate[:, cols]).astype(BF16)


def _in_proj_pool(x, wqv_t, w_rest, w_pool_bf, pool_scale):
    B, S, _ = x.shape
    tm, tk = SEQ_TILE, KEY_TILE
    nt = S // tm
    kern = functools.partial(_in_proj_pool_kernel, tm=tm, tk=tk)
    return pl.pallas_call(
        kern,
        grid=(B, nt),
        in_specs=[
            pl.BlockSpec((None, tm, D_MODEL), lambda b, i: (b, i, 0)),
            pl.BlockSpec((2 * ATTN_W, D_MODEL), lambda b, i: (0, 0)),
            pl.BlockSpec((D_MODEL, REST_W), lambda b, i: (0, 0)),
            pl.BlockSpec((N_POOL_GROUPS, POOL_CH, POOL_CH), lambda b, i: (0, 0, 0)),
            pl.BlockSpec((1, POOL_W), lambda b, i: (0, 0)),
        ],
        out_specs=[
            pl.BlockSpec((None, None, ATTN_W, 2 * tm), lambda b, i: (b, i, 0, 0)),
            pl.BlockSpec((None, tm, ATTN_W), lambda b, i: (b, i, 0)),
            pl.BlockSpec((None, tm // tk, N_HEADS * V_ROWS, tk), lambda b, i: (b, i, 0, 0)),
            pl.BlockSpec((None, tm, ATTN_W), lambda b, i: (b, i, 0)),
            pl.BlockSpec((None, tm, POOL_W), lambda b, i: (b, i, 0)),
        ],
        out_shape=[
            jax.ShapeDtypeStruct((B, nt, ATTN_W, 2 * tm), BF16),
            jax.ShapeDtypeStruct((B, S, ATTN_W), BF16),
            jax.ShapeDtypeStruct((B, S // tk, N_HEADS * V_ROWS, tk), BF16),
            jax.ShapeDtypeStruct((B, S, ATTN_W), F32),
            jax.ShapeDtypeStruct((B, S, POOL_W), BF16),
        ],
        scratch_shapes=[pltpu.VMEM((POOL_HALO + tm, POOL_W), F32)],
        compiler_params=pltpu.CompilerParams(
            dimension_semantics=("arbitrary", "arbitrary"),
            vmem_limit_bytes=V7X_VMEM_LIMIT_BYTES),
        name="in_proj_pool",
    )(x, wqv_t, w_rest, w_pool_bf, pool_scale)


def _diff_attn_kernel(lq1_ref, lk1_ref, lq2_ref, lk2_ref, subln_ref,
                      qz_ref, k_ref, vt_ref, g_ref, o_ref,
                      m_ref, acc_ref, s_ref, bias_ref, *, qt, tq, tk, lam_init):
    qi = pl.program_id(1)
    first_diag = qi * qt
    hq = tq // 2

    @pl.when((pl.program_id(0) == 0) & (qi == 0))
    def _():
        key = lax.broadcasted_iota(jnp.int32, (tk, tq), 0)
        col = lax.broadcasted_iota(jnp.int32, (tk, tq), 1)
        qry = hq + jnp.where(col >= hq, col - hq, col)
        bias_ref[0] = jnp.zeros((tk, tq), F32)
        bias_ref[1] = jnp.where(key <= qry, 0.0, -jnp.inf).astype(F32)

    m_ref[...] = jnp.full(m_ref.shape, -jnp.inf, F32)
    acc_ref[...] = jnp.zeros(acc_ref.shape, F32)

    def scores(j, unit, bias_slot, slot, first_keys_only=False):
        sub, h, half = unit
        qcols = slice(half * tq, (half + 1) * tq)
        nk = hq if first_keys_only else tk
        k_start = pl.multiple_of(j * tk, tk)
        kj = k_ref[pl.ds(k_start, nk), h * HEAD_W:(h + 1) * HEAD_W]
        s_t = jnp.dot(kj, qz_ref[sub, h * HEAD_W:(h + 1) * HEAD_W, qcols],
                      preferred_element_type=F32)
        if first_keys_only:
            s_t = s_t + bias_ref[1, hq:tk, :]
        elif bias_slot is not None:
            s_t = s_t + bias_ref[bias_slot]
        m_prev = m_ref[sub, h, :, qcols]
        m_new = jnp.maximum(m_prev, jnp.max(s_t, axis=0, keepdims=True))
        m_ref[sub, h, :, qcols] = m_new
        s_ref[slot, 0:nk, :] = s_t
        return m_prev, m_new

    def accumulate(j, unit, slot, m_prev, m_new, first_keys_only=False):
        sub, h, half = unit
        qcols = slice(half * tq, (half + 1) * tq)
        nk = hq if first_keys_only else tk
        vj = vt_ref[j, h * V_ROWS:(h + 1) * V_ROWS, 0:nk]
        alpha = jnp.exp2(m_prev - m_new)
        p_t = jnp.exp2(s_ref[slot, 0:nk, :] - m_new).astype(BF16)
        acc_ref[sub, h, :, qcols] = (alpha * acc_ref[sub, h, :, qcols]
                                     + jnp.dot(vj, p_t, preferred_element_type=F32))

    head_units = [(h, half) for h in range(N_HEADS) for half in (1, 0)]
    full_units = [(sub, h, half) for sub in range(qt) for (h, half) in head_units]
    first_unit = full_units[0]
    assert len(full_units) % 2 == 0

    def first_unit_bias_slot(j):
        return (j == first_diag).astype(jnp.int32)

    def body(j, pending):
        for idx, unit in enumerate(full_units):
            if idx + 1 < len(full_units):
                nxt = scores(j, full_units[idx + 1], None, (idx + 1) % 2)
            else:
                nxt = scores(j + 1, first_unit, first_unit_bias_slot(j + 1), 0)
            accumulate(j, unit, idx % 2, *pending)
            pending = nxt
        return pending

    pending = scores(0, first_unit, first_unit_bias_slot(0), 0)
    pending = lax.fori_loop(0, first_diag, body, pending)

    diag_items = []
    for d in range(qt):
        for sub in range(d, qt):
            for (h, half) in head_units:
                on_diag = sub == d
                diag_items.append((d, (sub, h, half),
                                   1 if on_diag and half == 1 else None,
                                   on_diag and half == 0))
    assert len(diag_items) % 2 == 0
    for idx, (d, unit, bias_slot, first_only) in enumerate(diag_items):
        if idx + 1 < len(diag_items):
            d2, unit2, bias2, first_only2 = diag_items[idx + 1]
            nxt = scores(first_diag + d2, unit2, bias2, (idx + 1) % 2, first_only2)
        else:
            nxt = None
        accumulate(first_diag + d, unit, idx % 2, *pending, first_keys_only=first_only)
        pending = nxt

    lam = (jnp.exp(jnp.sum(lq1_ref[...] * lk1_ref[...], axis=-1, keepdims=True))
           - jnp.exp(jnp.sum(lq2_ref[...] * lk2_ref[...], axis=-1, keepdims=True))
           + lam_init)
    for sub in range(qt):
        for h in range(N_HEADS):
            cols = slice(h * HEAD_W, (h + 1) * HEAD_W)
            acc = acc_ref[sub, h]
            inv_l = 1.0 / acc[HEAD_W:HEAD_W + 1, :]
            for half in range(2):
                rows = slice(sub * tq + half * hq, sub * tq + (half + 1) * hq)
                c1 = slice(half * tq, half * tq + hq)
                c2 = slice(half * tq + hq, (half + 1) * tq)
                o_t = (acc[0:HEAD_W, c1] * inv_l[:, c1]
                       - acc[0:HEAD_W, c2] * (lam * inv_l[:, c2]))
                ms = jnp.mean(o_t * o_t, axis=0, keepdims=True)
                o_t = o_t * (lax.rsqrt(ms + LN_EPS) * (1.0 - lam_init))
                o = o_t.T * subln_ref[...]
                o_ref[rows, cols] = (o * _silu(g_ref[rows, cols])).astype(BF16)


def _diff_attn(qz, k, vt, g_attn, lam_q1, lam_k1, lam_q2, lam_k2, subln_g, *, lam_init):
    B, S, _ = k.shape
    tq, tk, qt = SEQ_TILE, KEY_TILE, Q_SUBTILES
    assert tq == tk, "a diagonal block must be square for the shared causal bias tile"
    nt = S // tq
    kern = functools.partial(_diff_attn_kernel, qt=qt, tq=tq, tk=tk, lam_init=lam_init)
    vec = lambda n: pl.BlockSpec((1, n), lambda b, i: (0, 0))
    return pl.pallas_call(
        kern,
        grid=(B, nt // qt),
        in_specs=[
            vec(HEAD_DIM), vec(HEAD_DIM), vec(HEAD_DIM), vec(HEAD_DIM), vec(HEAD_W),
            pl.BlockSpec((None, qt, ATTN_W, 2 * tq), lambda b, i: (b, i, 0, 0)),
            pl.BlockSpec((None, S, ATTN_W), lambda b, i: (b, 0, 0)),
            pl.BlockSpec((None, S // tk, N_HEADS * V_ROWS, tk), lambda b, i: (b, 0, 0, 0)),
            pl.BlockSpec((None, qt * tq, ATTN_W), lambda b, i: (b, i, 0)),
        ],
        out_specs=pl.BlockSpec((None, qt * tq, ATTN_W), lambda b, i: (b, i, 0)),
        out_shape=jax.ShapeDtypeStruct((B, S, ATTN_W), BF16),
        scratch_shapes=[
            pltpu.VMEM((qt, N_HEADS, 1, 2 * tq), F32),
            pltpu.VMEM((qt, N_HEADS, V_ROWS, 2 * tq), F32),
            pltpu.VMEM((2, tk, tq), F32),
            pltpu.VMEM((2, tk, tq), F32),
        ],
        compiler_params=pltpu.CompilerParams(
            dimension_semantics=("arbitrary", "arbitrary"),
            vmem_limit_bytes=V7X_VMEM_LIMIT_BYTES),
        name="diff_attn",
    )(lam_q1, lam_k1, lam_q2, lam_k2, subln_g, qz, k, vt, g_attn)


def _out_proj_kernel(x_ref, attn_ref, pool_ref, p_ref, wout_ref, lng_ref, lnb_ref,
                     wpe_ref, wpg_ref, bpg_ref, o_ref, *, alpha, tm, chunk):
    def branch_matmuls(c):
        rows = slice(c * chunk, (c + 1) * chunk)
        mix = (jnp.dot(attn_ref[rows, :], wout_ref[0:ATTN_W, :], preferred_element_type=F32)
               + jnp.dot(pool_ref[rows, :], wout_ref[ATTN_W:, :], preferred_element_type=F32))
        pe = jnp.dot(p_ref[rows, :].astype(BF16), wpe_ref[...], preferred_element_type=F32)
        return mix, pe

    def norm_and_gate(c, mix, pe):
        rows = slice(c * chunk, (c + 1) * chunk)
        r = alpha * x_ref[rows, :] + mix
        mu = jnp.mean(r, axis=-1, keepdims=True)
        cen = r - mu
        var = jnp.mean(cen * cen, axis=-1, keepdims=True)
        h = cen * lax.rsqrt(var + LN_EPS) * lng_ref[...] + lnb_ref[...]
        logits = jnp.dot(h.astype(BF16), wpg_ref[...], preferred_element_type=F32) + bpg_ref[...]
        gate = 1.0 / (1.0 + jnp.exp(-logits))
        o_ref[rows, :] = h + pe * gate

    pending = branch_matmuls(0)
    for c in range(tm // chunk):
        nxt = branch_matmuls(c + 1) if (c + 1) * chunk < tm else None
        norm_and_gate(c, *pending)
        pending = nxt


def _out_proj(x2, attn2, pool2, p2, w_out_bf, ln_g, ln_b, w_pe_bf, w_pg_bf, b_pg, *, alpha):
    R = x2.shape[0]
    tm, chunk = OUT_TILE, OUT_CHUNK
    kern = functools.partial(_out_proj_kernel, alpha=alpha, tm=tm, chunk=chunk)
    rows = lambda n: pl.BlockSpec((tm, n), lambda i: (i, 0))
    full = lambda a, b: pl.BlockSpec((a, b), lambda i: (0, 0))
    return pl.pallas_call(
        kern,
        grid=(R // tm,),
        in_specs=[
            rows(D_MODEL), rows(ATTN_W), rows(POOL_W), rows(PLE_DIM),
            full(ATTN_W + POOL_W, D_MODEL), full(1, D_MODEL), full(1, D_MODEL),
            full(PLE_DIM, D_MODEL), full(D_MODEL, D_MODEL), full(1, D_MODEL),
        ],
        out_specs=rows(D_MODEL),
        out_shape=jax.ShapeDtypeStruct((R, D_MODEL), F32),
        compiler_params=pltpu.CompilerParams(
            dimension_semantics=("arbitrary",),
            vmem_limit_bytes=V7X_VMEM_LIMIT_BYTES),
        name="out_proj_ln_ple",
    )(x2, attn2, pool2, p2, w_out_bf, ln_g, ln_b, w_pe_bf, w_pg_bf, b_pg)


def kernel(x, p, w_in, lam_q1, lam_k1, lam_q2, lam_k2, subln_g, w_pool, pool_scale,
           w_out, ln_g, ln_b, w_pe, w_pg, b_pg):
    B, S, D = x.shape
    depth = w_in.shape[0]
    h = x
    for i in range(depth):
        alpha = (2.0 * depth) ** 0.25
        w = w_in[i]
        wqv_t = jnp.concatenate([w[:, 0:ATTN_W], w[:, 2 * ATTN_W:3 * ATTN_W]], axis=1).T.astype(BF16)
        w_rest = jnp.concatenate([w[:, ATTN_W:2 * ATTN_W], w[:, 3 * ATTN_W:]], axis=1).astype(BF16)
        qz, k, vt, g_attn, pool_mix = _in_proj_pool(
            h, wqv_t, w_rest, w_pool[i].astype(BF16), pool_scale[i][None, :])
        attn = _diff_attn(
            qz, k, vt, g_attn, lam_q1[i][None, :], lam_k1[i][None, :], lam_q2[i][None, :],
            lam_k2[i][None, :], subln_g[i][None, :], lam_init=_lambda_init(i))
        out = _out_proj(
            h.reshape(B * S, D), attn.reshape(B * S, ATTN_W), pool_mix.reshape(B * S, POOL_W),
            p[i].reshape(B * S, PLE_DIM), w_out[i].astype(BF16), ln_g[i][None, :],
            ln_b[i][None, :], w_pe[i].astype(BF16), w_pg[i].astype(BF16), b_pg[i][None, :],
            alpha=alpha)
        h = out.reshape(B, S, D)
    return h
```

```python
import functools
import math

import jax
import jax.numpy as jnp
from jax import lax
from jax.experimental import pallas as pl
from jax.experimental.pallas import tpu as pltpu

D_MODEL = 1024
PLE_DIM = 256
N_HEADS = 4
HEAD_DIM = 64
HEAD_W = 2 * HEAD_DIM
ATTN_W = N_HEADS * HEAD_W
N_POOL_GROUPS = 4
POOL_WINDOWS = (2, 4, 8, 16)
POOL_CH = 128
POOL_W = N_POOL_GROUPS * POOL_CH
REST_W = ATTN_W + ATTN_W + 2 * POOL_W
IN_W = 2 * ATTN_W + REST_W
LN_EPS = 1e-5
LOG2_E = math.log2(math.e)
POOL_HALO = max(POOL_WINDOWS)

BF16_SUBLANE_TILE = 16
V_ROWS = HEAD_W + BF16_SUBLANE_TILE

SEQ_TILE = 512
KEY_TILE = 512
Q_SUBTILES = 2
OUT_TILE = 1024
OUT_CHUNK = 256
V7X_VMEM_LIMIT_BYTES = 56 * 1024 * 1024

BF16 = jnp.bfloat16
F32 = jnp.float32
NT_DIMS = (((1,), (1,)), ((), ()))


def _lambda_init(layer_idx):
    return 0.8 - 0.6 * math.exp(-0.3 * layer_idx)


def _silu(g):
    return g / (1.0 + jnp.exp(-g))


def _in_proj_pool_kernel(x_ref, win_ref, wpool_ref, pscale_ref,
                         qz_ref, k_ref, vt_ref, gattn_ref, pool_ref,
                         ubuf_ref, wqv_ref, wrest_ref, *, tm, tk):
    i = pl.program_id(1)

    @pl.when((pl.program_id(0) == 0) & (i == 0))
    def _():
        wqv_ref[0:ATTN_W, :] = win_ref[:, 0:ATTN_W].T.astype(BF16)
        wqv_ref[ATTN_W:, :] = win_ref[:, 2 * ATTN_W:3 * ATTN_W].T.astype(BF16)
        wrest_ref[:, 0:ATTN_W] = win_ref[:, ATTN_W:2 * ATTN_W].astype(BF16)
        wrest_ref[:, ATTN_W:] = win_ref[:, 3 * ATTN_W:].astype(BF16)

    @pl.when(i == 0)
    def _():
        ubuf_ref[0:POOL_HALO, :] = jnp.zeros((POOL_HALO, POOL_W), F32)

    @pl.when(i > 0)
    def _():
        ubuf_ref[0:POOL_HALO, :] = ubuf_ref[tm:tm + POOL_HALO, :]

    xb = x_ref[...].astype(BF16)

    def proj(lo, width):
        return jnp.dot(xb, wrest_ref[:, lo:lo + width], preferred_element_type=F32)

    ubuf_ref[POOL_HALO:, :] = proj(2 * ATTN_W, POOL_W)
    pool_gate = pscale_ref[...] * _silu(proj(2 * ATTN_W + POOL_W, POOL_W))

    t = (i * tm + lax.broadcasted_iota(jnp.int32, (tm, 1), 0) + 1).astype(F32)
    deltas = []
    for g, w in enumerate(POOL_WINDOWS):
        ext = ubuf_ref[:, g * POOL_CH:(g + 1) * POOL_CH]
        acc = ext
        span = 1
        while span < w:
            acc = acc + pltpu.roll(acc, span, axis=0)
            span *= 2
        mean = acc[POOL_HALO:, :] / jnp.minimum(t, float(w))
        deltas.append((mean - ext[POOL_HALO:, :]).astype(BF16))

    scale = LOG2_E / math.sqrt(HEAD_DIM)
    q_t = lax.dot_general(wqv_ref[0:ATTN_W, :], xb, NT_DIMS, preferred_element_type=F32)
    v_t = lax.dot_general(wqv_ref[ATTN_W:, :], xb, NT_DIMS, preferred_element_type=F32)
    hq = tm // 2
    zeros = jnp.zeros((HEAD_DIM, hq), BF16)
    ones_tile = jnp.where(
        lax.broadcasted_iota(jnp.int32, (BF16_SUBLANE_TILE, tk), 0) == 0, 1.0, 0.0).astype(BF16)
    for h in range(N_HEADS):
        f0 = h * HEAD_W
        q1 = (q_t[f0:f0 + HEAD_DIM] * scale).astype(BF16)
        q2 = (q_t[f0 + HEAD_DIM:f0 + HEAD_W] * scale).astype(BF16)
        for half in range(2):
            qs = slice(half * hq, (half + 1) * hq)
            c1 = slice(half * tm, half * tm + hq)
            c2 = slice(half * tm + hq, (half + 1) * tm)
            qz_ref[f0:f0 + HEAD_DIM, c1] = q1[:, qs]
            qz_ref[f0 + HEAD_DIM:f0 + HEAD_W, c1] = zeros
            qz_ref[f0:f0 + HEAD_DIM, c2] = zeros
            qz_ref[f0 + HEAD_DIM:f0 + HEAD_W, c2] = q2[:, qs]
        r0 = h * V_ROWS
        for c in range(tm // tk):
            vt_ref[c, r0:r0 + HEAD_W, :] = v_t[f0:f0 + HEAD_W, c * tk:(c + 1) * tk].astype(BF16)
            vt_ref[c, r0 + HEAD_W:r0 + V_ROWS, :] = ones_tile

    k_ref[...] = proj(0, ATTN_W).astype(BF16)
    gattn_ref[...] = proj(ATTN_W, ATTN_W)

    for g in range(N_POOL_GROUPS):
        cols = slice(g * POOL_CH, (g + 1) * POOL_CH)
        y = jnp.dot(deltas[g], wpool_ref[g], preferred_element_type=F32)
        pool_ref[:, cols] = (y * pool_gate[:, cols]).astype(BF16)


def _in_proj_pool(x, w_in, layer, w_pool_bf, pool_scale):
    B, S, _ = x.shape
    tm, tk = SEQ_TILE, KEY_TILE
    nt = S // tm
    kern = functools.partial(_in_proj_pool_kernel, tm=tm, tk=tk)
    return pl.pallas_call(
        kern,
        grid=(B, nt),
        in_specs=[
            pl.BlockSpec((None, tm, D_MODEL), lambda b, i: (b, i, 0)),
            pl.BlockSpec((None, D_MODEL, IN_W), lambda b, i: (layer, 0, 0),
                         pipeline_mode=pl.Buffered(1)),
            pl.BlockSpec((N_POOL_GROUPS, POOL_CH, POOL_CH), lambda b, i: (0, 0, 0)),
            pl.BlockSpec((1, POOL_W), lambda b, i: (0, 0)),
        ],
        out_specs=[
            pl.BlockSpec((None, None, ATTN_W, 2 * tm), lambda b, i: (b, i, 0, 0)),
            pl.BlockSpec((None, tm, ATTN_W), lambda b, i: (b, i, 0)),
            pl.BlockSpec((None, tm // tk, N_HEADS * V_ROWS, tk), lambda b, i: (b, i, 0, 0)),
            pl.BlockSpec((None, tm, ATTN_W), lambda b, i: (b, i, 0)),
            pl.BlockSpec((None, tm, POOL_W), lambda b, i: (b, i, 0)),
        ],
        out_shape=[
            jax.ShapeDtypeStruct((B, nt, ATTN_W, 2 * tm), BF16),
            jax.ShapeDtypeStruct((B, S, ATTN_W), BF16),
            jax.ShapeDtypeStruct((B, S // tk, N_HEADS * V_ROWS, tk), BF16),
            jax.ShapeDtypeStruct((B, S, ATTN_W), F32),
            jax.ShapeDtypeStruct((B, S, POOL_W), BF16),
        ],
        scratch_shapes=[
            pltpu.VMEM((POOL_HALO + tm, POOL_W), F32),
            pltpu.VMEM((2 * ATTN_W, D_MODEL), BF16),
            pltpu.VMEM((D_MODEL, REST_W), BF16),
        ],
        compiler_params=pltpu.CompilerParams(
            dimension_semantics=("arbitrary", "arbitrary"),
            vmem_limit_bytes=V7X_VMEM_LIMIT_BYTES),
        name="in_proj_pool",
    )(x, w_in, w_pool_bf, pool_scale)


def _diff_attn_kernel(lq1_ref, lk1_ref, lq2_ref, lk2_ref, subln_ref,
                      qz_ref, k_ref, vt_ref, g_ref, o_ref,
                      m_ref, acc_ref, s_ref, bias_ref, *, qt, tq, tk, lam_init):
    qi = pl.program_id(1)
    first_diag = qi * qt
    hq = tq // 2

    @pl.when((pl.program_id(0) == 0) & (qi == 0))
    def _():
        key = lax.broadcasted_iota(jnp.int32, (tk, tq), 0)
        col = lax.broadcasted_iota(jnp.int32, (tk, tq), 1)
        qry = hq + jnp.where(col >= hq, col - hq, col)
        bias_ref[0] = jnp.zeros((tk, tq), F32)
        bias_ref[1] = jnp.where(key <= qry, 0.0, -jnp.inf).astype(F32)

    m_ref[...] = jnp.full(m_ref.shape, -jnp.inf, F32)
    acc_ref[...] = jnp.zeros(acc_ref.shape, F32)

    def scores(j, unit, bias_slot, slot, first_keys_only=False):
        sub, h, half = unit
        qcols = slice(half * tq, (half + 1) * tq)
        nk = hq if first_keys_only else tk
        k_start = pl.multiple_of(j * tk, tk)
        kj = k_ref[pl.ds(k_start, nk), h * HEAD_W:(h + 1) * HEAD_W]
        s_t = jnp.dot(kj, qz_ref[sub, h * HEAD_W:(h + 1) * HEAD_W, qcols],
                      preferred_element_type=F32)
        if first_keys_only:
            s_t = s_t + bias_ref[1, hq:tk, :]
        elif bias_slot is not None:
            s_t = s_t + bias_ref[bias_slot]
        m_prev = m_ref[sub, h, :, qcols]
        m_new = jnp.maximum(m_prev, jnp.max(s_t, axis=0, keepdims=True))
        m_ref[sub, h, :, qcols] = m_new
        s_ref[slot, 0:nk, :] = s_t
        return m_prev, m_new

    def accumulate(j, unit, slot, m_prev, m_new, first_keys_only=False):
        sub, h, half = unit
        qcols = slice(half * tq, (half + 1) * tq)
        nk = hq if first_keys_only else tk
        vj = vt_ref[j, h * V_ROWS:(h + 1) * V_ROWS, 0:nk]
        alpha = jnp.exp2(m_prev - m_new)
        p_t = jnp.exp2(s_ref[slot, 0:nk, :] - m_new).astype(BF16)
        acc_ref[sub, h, :, qcols] = (alpha * acc_ref[sub, h, :, qcols]
                                     + jnp.dot(vj, p_t, preferred_element_type=F32))

    head_units = [(h, half) for h in range(N_HEADS) for half in (1, 0)]
    full_units = [(sub, h, half) for sub in range(qt) for (h, half) in head_units]
    first_unit = full_units[0]
    assert len(full_units) % 2 == 0

    def first_unit_bias_slot(j):
        return (j == first_diag).astype(jnp.int32)

    def body(j, pending):
        for idx, unit in enumerate(full_units):
            if idx + 1 < len(full_units):
                nxt = scores(j, full_units[idx + 1], None, (idx + 1) % 2)
            else:
                nxt = scores(j + 1, first_unit, first_unit_bias_slot(j + 1), 0)
            accumulate(j, unit, idx % 2, *pending)
            pending = nxt
        return pending

    pending = scores(0, first_unit, first_unit_bias_slot(0), 0)
    pending = lax.fori_loop(0, first_diag, body, pending)

    diag_items = []
    for d in range(qt):
        for sub in range(d, qt):
            for (h, half) in head_units:
                on_diag = sub == d
                diag_items.append((d, (sub, h, half),
                                   1 if on_diag and half == 1 else None,
                                   on_diag and half == 0))
    assert len(diag_items) % 2 == 0
    for idx, (d, unit, bias_slot, first_only) in enumerate(diag_items):
        if idx + 1 < len(diag_items):
            d2, unit2, bias2, first_only2 = diag_items[idx + 1]
            nxt = scores(first_diag + d2, unit2, bias2, (idx + 1) % 2, first_only2)
        else:
            nxt = None
        accumulate(first_diag + d, unit, idx % 2, *pending, first_keys_only=first_only)
        pending = nxt

    lam = (jnp.exp(jnp.sum(lq1_ref[...] * lk1_ref[...], axis=-1, keepdims=True))
           - jnp.exp(jnp.sum(lq2_ref[...] * lk2_ref[...], axis=-1, keepdims=True))
           + lam_init)
    for sub in range(qt):
        for h in range(N_HEADS):
            cols = slice(h * HEAD_W, (h + 1) * HEAD_W)
            acc = acc_ref[sub, h]
            inv_l = 1.0 / acc[HEAD_W:HEAD_W + 1, :]
            for half in range(2):
                rows = slice(sub * tq + half * hq, sub * tq + (half + 1) * hq)
                c1 = slice(half * tq, half * tq + hq)
                c2 = slice(half * tq + hq, (half + 1) * tq)
                o_t = (acc[0:HEAD_W, c1] * inv_l[:, c1]
                       - acc[0:HEAD_W, c2] * (lam * inv_l[:, c2]))
                ms = jnp.mean(o_t * o_t, axis=0, keepdims=True)
                o_t = o_t * (lax.rsqrt(ms + LN_EPS) * (1.0 - lam_init))
                o = o_t.T * subln_ref[...]
                o_ref[rows, cols] = (o * _silu(g_ref[rows, cols])).astype(BF16)


def _diff_attn(qz, k, vt, g_attn, lam_q1, lam_k1, lam_q2, lam_k2, subln_g, *, lam_init):
    B, S, _ = k.shape
    tq, tk, qt = SEQ_TILE, KEY_TILE, Q_SUBTILES
    assert tq == tk, "a diagonal block must be square for the shared causal bias tile"
    nt = S // tq
    kern = functools.partial(_diff_attn_kernel, qt=qt, tq=tq, tk=tk, lam_init=lam_init)
    vec = lambda n: pl.BlockSpec((1, n), lambda b, i: (0, 0))
    return pl.pallas_call(
        kern,
        grid=(B, nt // qt),
        in_specs=[
            vec(HEAD_DIM), vec(HEAD_DIM), vec(HEAD_DIM), vec(HEAD_DIM), vec(HEAD_W),
            pl.BlockSpec((None, qt, ATTN_W, 2 * tq), lambda b, i: (b, i, 0, 0)),
            pl.BlockSpec((None, S, ATTN_W), lambda b, i: (b, 0, 0)),
            pl.BlockSpec((None, S // tk, N_HEADS * V_ROWS, tk), lambda b, i: (b, 0, 0, 0)),
            pl.BlockSpec((None, qt * tq, ATTN_W), lambda b, i: (b, i, 0)),
        ],
        out_specs=pl.BlockSpec((None, qt * tq, ATTN_W), lambda b, i: (b, i, 0)),
        out_shape=jax.ShapeDtypeStruct((B, S, ATTN_W), BF16),
        scratch_shapes=[
            pltpu.VMEM((qt, N_HEADS, 1, 2 * tq), F32),
            pltpu.VMEM((qt, N_HEADS, V_ROWS, 2 * tq), F32),
            pltpu.VMEM((2, tk, tq), F32),
            pltpu.VMEM((2, tk, tq), F32),
        ],
        compiler_params=pltpu.CompilerParams(
            dimension_semantics=("arbitrary", "arbitrary"),
            vmem_limit_bytes=V7X_VMEM_LIMIT_BYTES),
        name="diff_attn",
    )(lam_q1, lam_k1, lam_q2, lam_k2, subln_g, qz, k, vt, g_attn)


def _out_proj_kernel(x_ref, attn_ref, pool_ref, p_ref, wout_ref, lng_ref, lnb_ref,
                     wpe_ref, wpg_ref, bpg_ref, o_ref, *, alpha, tm, chunk):
    def branch_matmuls(c):
        rows = slice(c * chunk, (c + 1) * chunk)
        mix = (jnp.dot(attn_ref[rows, :], wout_ref[0:ATTN_W, :], preferred_element_type=F32)
               + jnp.dot(pool_ref[rows, :], wout_ref[ATTN_W:, :], preferred_element_type=F32))
        pe = jnp.dot(p_ref[rows, :].astype(BF16), wpe_ref[...], preferred_element_type=F32)
        return mix, pe

    def norm_and_gate(c, mix, pe):
        rows = slice(c * chunk, (c + 1) * chunk)
        r = alpha * x_ref[rows, :] + mix
        mu = jnp.mean(r, axis=-1, keepdims=True)
        cen = r - mu
        var = jnp.mean(cen * cen, axis=-1, keepdims=True)
        h = cen * lax.rsqrt(var + LN_EPS) * lng_ref[...] + lnb_ref[...]
        logits = jnp.dot(h.astype(BF16), wpg_ref[...], preferred_element_type=F32) + bpg_ref[...]
        gate = 1.0 / (1.0 + jnp.exp(-logits))
        o_ref[rows, :] = h + pe * gate

    pending = branch_matmuls(0)
    for c in range(tm // chunk):
        nxt = branch_matmuls(c + 1) if (c + 1) * chunk < tm else None
        norm_and_gate(c, *pending)
        pending = nxt


def _out_proj(x2, attn2, pool2, p2, w_out_bf, ln_g, ln_b, w_pe_bf, w_pg_bf, b_pg, *, alpha):
    R = x2.shape[0]
    tm, chunk = OUT_TILE, OUT_CHUNK
    kern = functools.partial(_out_proj_kernel, alpha=alpha, tm=tm, chunk=chunk)
    rows = lambda n: pl.BlockSpec((tm, n), lambda i: (i, 0))
    full = lambda a, b: pl.BlockSpec((a, b), lambda i: (0, 0))
    return pl.pallas_call(
        kern,
        grid=(R // tm,),
        in_specs=[
            rows(D_MODEL), rows(ATTN_W), rows(POOL_W), rows(PLE_DIM),
            full(ATTN_W + POOL_W, D_MODEL), full(1, D_MODEL), full(1, D_MODEL),
            full(PLE_DIM, D_MODEL), full(D_MODEL, D_MODEL), full(1, D_MODEL),
        ],
        out_specs=rows(D_MODEL),
        out_shape=jax.ShapeDtypeStruct((R, D_MODEL), F32),
        compiler_params=pltpu.CompilerParams(
            dimension_semantics=("arbitrary",),
            vmem_limit_bytes=V7X_VMEM_LIMIT_BYTES),
        name="out_proj_ln_ple",
    )(x2, attn2, pool2, p2, w_out_bf, ln_g, ln_b, w_pe_bf, w_pg_bf, b_pg)


def kernel(x, p, w_in, lam_q1, lam_k1, lam_q2, lam_k2, subln_g, w_pool, pool_scale,
           w_out, ln_g, ln_b, w_pe, w_pg, b_pg):
    B, S, D = x.shape
    depth = w_in.shape[0]
    h = x
    for i in range(depth):
        alpha = (2.0 * depth) ** 0.25
        qz, k, vt, g_attn, pool_mix = _in_proj_pool(
            h, w_in, i, w_pool[i].astype(BF16), pool_scale[i][None, :])
        attn = _diff_attn(
            qz, k, vt, g_attn, lam_q1[i][None, :], lam_k1[i][None, :], lam_q2[i][None, :],
            lam_k2[i][None, :], subln_g[i][None, :], lam_init=_lambda_init(i))
        out = _out_proj(
            h.reshape(B * S, D), attn.reshape(B * S, ATTN_W), pool_mix.reshape(B * S, POOL_W),
            p[i].reshape(B * S, PLE_DIM), w_out[i].astype(BF16), ln_g[i][None, :],
            ln_b[i][None, :], w_pe[i].astype(BF16), w_pg[i].astype(BF16), b_pg[i][None, :],
            alpha=alpha)
        h = out.reshape(B, S, D)
    return h
```

```python
import functools
import math

import jax
import jax.numpy as jnp
from jax import lax
from jax.experimental import pallas as pl
from jax.experimental.pallas import tpu as pltpu

D_MODEL = 1024
PLE_DIM = 256
N_HEADS = 4
HEAD_DIM = 64
HEAD_W = 2 * HEAD_DIM
ATTN_W = N_HEADS * HEAD_W
N_POOL_GROUPS = 4
POOL_WINDOWS = (2, 4, 8, 16)
POOL_CH = 128
POOL_W = N_POOL_GROUPS * POOL_CH
REST_W = ATTN_W + ATTN_W + 2 * POOL_W
IN_W = 2 * ATTN_W + REST_W
LN_EPS = 1e-5
LOG2_E = math.log2(math.e)
POOL_HALO = max(POOL_WINDOWS)

BF16_SUBLANE_TILE = 16
V_ROWS = HEAD_W + BF16_SUBLANE_TILE

SEQ_TILE = 512
KEY_TILE = 512
Q_SUBTILES = 2
OUT_TILE = 1024
OUT_CHUNK = 256
V7X_VMEM_LIMIT_BYTES = 56 * 1024 * 1024

BF16 = jnp.bfloat16
F32 = jnp.float32
NT_DIMS = (((1,), (1,)), ((), ()))


def _lambda_init(layer_idx):
    return 0.8 - 0.6 * math.exp(-0.3 * layer_idx)


def _silu(g):
    return g / (1.0 + jnp.exp(-g))


def _in_proj_pool_kernel(x_ref, win_ref, wpool_ref, pscale_ref,
                         qz_ref, k_ref, vt_ref, gattn_ref, pool_ref,
                         ubuf_ref, wqv_ref, wrest_ref, *, tm, tk):
    i = pl.program_id(1)

    @pl.when((pl.program_id(0) == 0) & (i == 0))
    def _():
        wqv_ref[0:ATTN_W, :] = win_ref[:, 0:ATTN_W].T.astype(BF16)
        wqv_ref[ATTN_W:, :] = win_ref[:, 2 * ATTN_W:3 * ATTN_W].T.astype(BF16)
        wrest_ref[:, 0:ATTN_W] = win_ref[:, ATTN_W:2 * ATTN_W].astype(BF16)
        wrest_ref[:, ATTN_W:] = win_ref[:, 3 * ATTN_W:].astype(BF16)

    @pl.when(i == 0)
    def _():
        ubuf_ref[0:POOL_HALO, :] = jnp.zeros((POOL_HALO, POOL_W), F32)

    @pl.when(i > 0)
    def _():
        ubuf_ref[0:POOL_HALO, :] = ubuf_ref[tm:tm + POOL_HALO, :]

    xb = x_ref[...].astype(BF16)

    def proj(lo, width):
        return jnp.dot(xb, wrest_ref[:, lo:lo + width], preferred_element_type=F32)

    ubuf_ref[POOL_HALO:, :] = proj(2 * ATTN_W, POOL_W)
    pool_gate = pscale_ref[...] * _silu(proj(2 * ATTN_W + POOL_W, POOL_W))

    t = (i * tm + lax.broadcasted_iota(jnp.int32, (tm, 1), 0) + 1).astype(F32)
    deltas = []
    for g, w in enumerate(POOL_WINDOWS):
        ext = ubuf_ref[:, g * POOL_CH:(g + 1) * POOL_CH]
        acc = ext
        span = 1
        while span < w:
            acc = acc + pltpu.roll(acc, span, axis=0)
            span *= 2
        mean = acc[POOL_HALO:, :] / jnp.minimum(t, float(w))
        deltas.append((mean - ext[POOL_HALO:, :]).astype(BF16))

    scale = LOG2_E / math.sqrt(HEAD_DIM)
    q_t = lax.dot_general(wqv_ref[0:ATTN_W, :], xb, NT_DIMS, preferred_element_type=F32)
    v_t = lax.dot_general(wqv_ref[ATTN_W:, :], xb, NT_DIMS, preferred_element_type=F32)
    hq = tm // 2
    zeros = jnp.zeros((HEAD_DIM, hq), BF16)
    ones_tile = jnp.where(
        lax.broadcasted_iota(jnp.int32, (BF16_SUBLANE_TILE, tk), 0) == 0, 1.0, 0.0).astype(BF16)
    for h in range(N_HEADS):
        f0 = h * HEAD_W
        q1 = (q_t[f0:f0 + HEAD_DIM] * scale).astype(BF16)
        q2 = (q_t[f0 + HEAD_DIM:f0 + HEAD_W] * scale).astype(BF16)
        for half in range(2):
            qs = slice(half * hq, (half + 1) * hq)
            c1 = slice(half * tm, half * tm + hq)
            c2 = slice(half * tm + hq, (half + 1) * tm)
            qz_ref[f0:f0 + HEAD_DIM, c1] = q1[:, qs]
            qz_ref[f0 + HEAD_DIM:f0 + HEAD_W, c1] = zeros
            qz_ref[f0:f0 + HEAD_DIM, c2] = zeros
            qz_ref[f0 + HEAD_DIM:f0 + HEAD_W, c2] = q2[:, qs]
        r0 = h * V_ROWS
        for c in range(tm // tk):
            vt_ref[c, r0:r0 + HEAD_W, :] = v_t[f0:f0 + HEAD_W, c * tk:(c + 1) * tk].astype(BF16)
            vt_ref[c, r0 + HEAD_W:r0 + V_ROWS, :] = ones_tile

    k_ref[...] = proj(0, ATTN_W).astype(BF16)
    gattn_ref[...] = proj(ATTN_W, ATTN_W)

    for g in range(N_POOL_GROUPS):
        cols = slice(g * POOL_CH, (g + 1) * POOL_CH)
        y = jnp.dot(deltas[g], wpool_ref[g].astype(BF16), preferred_element_type=F32)
        pool_ref[:, cols] = (y * pool_gate[:, cols]).astype(BF16)


def _in_proj_pool(x, w_in, w_pool, layer, pool_scale):
    B, S, _ = x.shape
    tm, tk = SEQ_TILE, KEY_TILE
    nt = S // tm
    kern = functools.partial(_in_proj_pool_kernel, tm=tm, tk=tk)
    return pl.pallas_call(
        kern,
        grid=(B, nt),
        in_specs=[
            pl.BlockSpec((None, tm, D_MODEL), lambda b, i: (b, i, 0)),
            pl.BlockSpec((None, D_MODEL, IN_W), lambda b, i: (layer, 0, 0),
                         pipeline_mode=pl.Buffered(1)),
            pl.BlockSpec((None, N_POOL_GROUPS, POOL_CH, POOL_CH), lambda b, i: (layer, 0, 0, 0)),
            pl.BlockSpec((1, POOL_W), lambda b, i: (0, 0)),
        ],
        out_specs=[
            pl.BlockSpec((None, None, ATTN_W, 2 * tm), lambda b, i: (b, i, 0, 0)),
            pl.BlockSpec((None, tm, ATTN_W), lambda b, i: (b, i, 0)),
            pl.BlockSpec((None, tm // tk, N_HEADS * V_ROWS, tk), lambda b, i: (b, i, 0, 0)),
            pl.BlockSpec((None, tm, ATTN_W), lambda b, i: (b, i, 0)),
            pl.BlockSpec((None, tm, POOL_W), lambda b, i: (b, i, 0)),
        ],
        out_shape=[
            jax.ShapeDtypeStruct((B, nt, ATTN_W, 2 * tm), BF16),
            jax.ShapeDtypeStruct((B, S, ATTN_W), BF16),
            jax.ShapeDtypeStruct((B, S // tk, N_HEADS * V_ROWS, tk), BF16),
            jax.ShapeDtypeStruct((B, S, ATTN_W), F32),
            jax.ShapeDtypeStruct((B, S, POOL_W), BF16),
        ],
        scratch_shapes=[
            pltpu.VMEM((POOL_HALO + tm, POOL_W), F32),
            pltpu.VMEM((2 * ATTN_W, D_MODEL), BF16),
            pltpu.VMEM((D_MODEL, REST_W), BF16),
        ],
        compiler_params=pltpu.CompilerParams(
            dimension_semantics=("arbitrary", "arbitrary"),
            vmem_limit_bytes=V7X_VMEM_LIMIT_BYTES),
        name="in_proj_pool",
    )(x, w_in, w_pool, pool_scale)


def _diff_attn_kernel(lq1_ref, lk1_ref, lq2_ref, lk2_ref, subln_ref,
                      qz_ref, k_ref, vt_ref, g_ref, o_ref,
                      m_ref, acc_ref, s_ref, bias_ref, *, qt, tq, tk, lam_init):
    qi = pl.program_id(1)
    first_diag = qi * qt
    hq = tq // 2

    @pl.when((pl.program_id(0) == 0) & (qi == 0))
    def _():
        key = lax.broadcasted_iota(jnp.int32, (tk, tq), 0)
        col = lax.broadcasted_iota(jnp.int32, (tk, tq), 1)
        qry = hq + jnp.where(col >= hq, col - hq, col)
        bias_ref[0] = jnp.zeros((tk, tq), F32)
        bias_ref[1] = jnp.where(key <= qry, 0.0, -jnp.inf).astype(F32)

    m_ref[...] = jnp.full(m_ref.shape, -jnp.inf, F32)
    acc_ref[...] = jnp.zeros(acc_ref.shape, F32)

    def scores(j, unit, bias_slot, slot, first_keys_only=False):
        sub, h, half = unit
        qcols = slice(half * tq, (half + 1) * tq)
        nk = hq if first_keys_only else tk
        k_start = pl.multiple_of(j * tk, tk)
        kj = k_ref[pl.ds(k_start, nk), h * HEAD_W:(h + 1) * HEAD_W]
        s_t = jnp.dot(kj, qz_ref[sub, h * HEAD_W:(h + 1) * HEAD_W, qcols],
                      preferred_element_type=F32)
        if first_keys_only:
            s_t = s_t + bias_ref[1, hq:tk, :]
        elif bias_slot is not None:
            s_t = s_t + bias_ref[bias_slot]
        m_prev = m_ref[sub, h, :, qcols]
        m_new = jnp.maximum(m_prev, jnp.max(s_t, axis=0, keepdims=True))
        m_ref[sub, h, :, qcols] = m_new
        s_ref[slot, 0:nk, :] = s_t
        return m_prev, m_new

    def accumulate(j, unit, slot, m_prev, m_new, first_keys_only=False):
        sub, h, half = unit
        qcols = slice(half * tq, (half + 1) * tq)
        nk = hq if first_keys_only else tk
        vj = vt_ref[j, h * V_ROWS:(h + 1) * V_ROWS, 0:nk]
        alpha = jnp.exp2(m_prev - m_new)
        p_t = jnp.exp2(s_ref[slot, 0:nk, :] - m_new).astype(BF16)
        acc_ref[sub, h, :, qcols] = (alpha * acc_ref[sub, h, :, qcols]
                                     + jnp.dot(vj, p_t, preferred_element_type=F32))

    head_units = [(h, half) for h in range(N_HEADS) for half in (1, 0)]
    full_units = [(sub, h, half) for sub in range(qt) for (h, half) in head_units]
    first_unit = full_units[0]
    assert len(full_units) % 2 == 0

    def first_unit_bias_slot(j):
        return (j == first_diag).astype(jnp.int32)

    def body(j, pending):
        for idx, unit in enumerate(full_units):
            if idx + 1 < len(full_units):
                nxt = scores(j, full_units[idx + 1], None, (idx + 1) % 2)
            else:
                nxt = scores(j + 1, first_unit, first_unit_bias_slot(j + 1), 0)
            accumulate(j, unit, idx % 2, *pending)
            pending = nxt
        return pending

    pending = scores(0, first_unit, first_unit_bias_slot(0), 0)
    pending = lax.fori_loop(0, first_diag, body, pending)

    diag_items = []
    for d in range(qt):
        for sub in range(d, qt):
            for (h, half) in head_units:
                on_diag = sub == d
                diag_items.append((d, (sub, h, half),
                                   1 if on_diag and half == 1 else None,
                                   on_diag and half == 0))
    assert len(diag_items) % 2 == 0
    for idx, (d, unit, bias_slot, first_only) in enumerate(diag_items):
        if idx + 1 < len(diag_items):
            d2, unit2, bias2, first_only2 = diag_items[idx + 1]
            nxt = scores(first_diag + d2, unit2, bias2, (idx + 1) % 2, first_only2)
        else:
            nxt = None
        accumulate(first_diag + d, unit, idx % 2, *pending, first_keys_only=first_only)
        pending = nxt

    lam = (jnp.exp(jnp.sum(lq1_ref[...] * lk1_ref[...], axis=-1, keepdims=True))
           - jnp.exp(jnp.sum(lq2_ref[...] * lk2_ref[...], axis=-1, keepdims=True))
           + lam_init)
    for sub in range(qt):
        for h in range(N_HEADS):
            cols = slice(h * HEAD_W, (h + 1) * HEAD_W)
            acc = acc_ref[sub, h]
            inv_l = 1.0 / acc[HEAD_W:HEAD_W + 1, :]
            for half in range(2):
                rows = slice(sub * tq + half * hq, sub * tq + (half + 1) * hq)
                c1 = slice(half * tq, half * tq + hq)
                c2 = slice(half * tq + hq, (half + 1) * tq)
                o_t = (acc[0:HEAD_W, c1] * inv_l[:, c1]
                       - acc[0:HEAD_W, c2] * (lam * inv_l[:, c2]))
                ms = jnp.mean(o_t * o_t, axis=0, keepdims=True)
                o_t = o_t * (lax.rsqrt(ms + LN_EPS) * (1.0 - lam_init))
                o = o_t.T * subln_ref[...]
                o_ref[rows, cols] = (o * _silu(g_ref[rows, cols])).astype(BF16)


def _diff_attn(qz, k, vt, g_attn, lam_q1, lam_k1, lam_q2, lam_k2, subln_g, *, lam_init):
    B, S, _ = k.shape
    tq, tk, qt = SEQ_TILE, KEY_TILE, Q_SUBTILES
    assert tq == tk, "a diagonal block must be square for the shared causal bias tile"
    nt = S // tq
    kern = functools.partial(_diff_attn_kernel, qt=qt, tq=tq, tk=tk, lam_init=lam_init)
    vec = lambda n: pl.BlockSpec((1, n), lambda b, i: (0, 0))
    return pl.pallas_call(
        kern,
        grid=(B, nt // qt),
        in_specs=[
            vec(HEAD_DIM), vec(HEAD_DIM), vec(HEAD_DIM), vec(HEAD_DIM), vec(HEAD_W),
            pl.BlockSpec((None, qt, ATTN_W, 2 * tq), lambda b, i: (b, i, 0, 0)),
            pl.BlockSpec((None, S, ATTN_W), lambda b, i: (b, 0, 0)),
            pl.BlockSpec((None, S // tk, N_HEADS * V_ROWS, tk), lambda b, i: (b, 0, 0, 0)),
            pl.BlockSpec((None, qt * tq, ATTN_W), lambda b, i: (b, i, 0)),
        ],
        out_specs=pl.BlockSpec((None, qt * tq, ATTN_W), lambda b, i: (b, i, 0)),
        out_shape=jax.ShapeDtypeStruct((B, S, ATTN_W), BF16),
        scratch_shapes=[
            pltpu.VMEM((qt, N_HEADS, 1, 2 * tq), F32),
            pltpu.VMEM((qt, N_HEADS, V_ROWS, 2 * tq), F32),
            pltpu.VMEM((2, tk, tq), F32),
            pltpu.VMEM((2, tk, tq), F32),
        ],
        compiler_params=pltpu.CompilerParams(
            dimension_semantics=("arbitrary", "arbitrary"),
            vmem_limit_bytes=V7X_VMEM_LIMIT_BYTES),
        name="diff_attn",
    )(lam_q1, lam_k1, lam_q2, lam_k2, subln_g, qz, k, vt, g_attn)


def _out_proj_kernel(x_ref, attn_ref, pool_ref, p_ref, wout32_ref, lng_ref, lnb_ref,
                     wpe32_ref, wpg32_ref, bpg_ref, o_ref, wout_ref, wpe_ref, wpg_ref,
                     *, alpha, tm, chunk):
    @pl.when(pl.program_id(0) == 0)
    def _():
        wout_ref[...] = wout32_ref[...].astype(BF16)
        wpe_ref[...] = wpe32_ref[...].astype(BF16)
        wpg_ref[...] = wpg32_ref[...].astype(BF16)

    def branch_matmuls(c):
        rows = slice(c * chunk, (c + 1) * chunk)
        mix = (jnp.dot(attn_ref[rows, :], wout_ref[0:ATTN_W, :], preferred_element_type=F32)
               + jnp.dot(pool_ref[rows, :], wout_ref[ATTN_W:, :], preferred_element_type=F32))
        pe = jnp.dot(p_ref[rows, :].astype(BF16), wpe_ref[...], preferred_element_type=F32)
        return mix, pe

    def norm_and_gate(c, mix, pe):
        rows = slice(c * chunk, (c + 1) * chunk)
        r = alpha * x_ref[rows, :] + mix
        mu = jnp.mean(r, axis=-1, keepdims=True)
        cen = r - mu
        var = jnp.mean(cen * cen, axis=-1, keepdims=True)
        h = cen * lax.rsqrt(var + LN_EPS) * lng_ref[...] + lnb_ref[...]
        logits = jnp.dot(h.astype(BF16), wpg_ref[...], preferred_element_type=F32) + bpg_ref[...]
        gate = 1.0 / (1.0 + jnp.exp(-logits))
        o_ref[rows, :] = h + pe * gate

    pending = branch_matmuls(0)
    for c in range(tm // chunk):
        nxt = branch_matmuls(c + 1) if (c + 1) * chunk < tm else None
        norm_and_gate(c, *pending)
        pending = nxt


def _out_proj(x2, attn2, pool2, p2, w_out, ln_g, ln_b, w_pe, w_pg, b_pg, layer, *, alpha):
    R = x2.shape[0]
    tm, chunk = OUT_TILE, OUT_CHUNK
    kern = functools.partial(_out_proj_kernel, alpha=alpha, tm=tm, chunk=chunk)
    rows = lambda n: pl.BlockSpec((tm, n), lambda i: (i, 0))
    full = lambda a, b: pl.BlockSpec((a, b), lambda i: (0, 0))
    once = lambda a, b: pl.BlockSpec((None, a, b), lambda i: (layer, 0, 0),
                                     pipeline_mode=pl.Buffered(1))
    return pl.pallas_call(
        kern,
        grid=(R // tm,),
        in_specs=[
            rows(D_MODEL), rows(ATTN_W), rows(POOL_W), rows(PLE_DIM),
            once(ATTN_W + POOL_W, D_MODEL), full(1, D_MODEL), full(1, D_MODEL),
            once(PLE_DIM, D_MODEL), once(D_MODEL, D_MODEL), full(1, D_MODEL),
        ],
        out_specs=rows(D_MODEL),
        out_shape=jax.ShapeDtypeStruct((R, D_MODEL), F32),
        scratch_shapes=[
            pltpu.VMEM((ATTN_W + POOL_W, D_MODEL), BF16),
            pltpu.VMEM((PLE_DIM, D_MODEL), BF16),
            pltpu.VMEM((D_MODEL, D_MODEL), BF16),
        ],
        compiler_params=pltpu.CompilerParams(
            dimension_semantics=("arbitrary",),
            vmem_limit_bytes=V7X_VMEM_LIMIT_BYTES),
        name="out_proj_ln_ple",
    )(x2, attn2, pool2, p2, w_out, ln_g, ln_b, w_pe, w_pg, b_pg)


def kernel(x, p, w_in, lam_q1, lam_k1, lam_q2, lam_k2, subln_g, w_pool, pool_scale,
           w_out, ln_g, ln_b, w_pe, w_pg, b_pg):
    B, S, D = x.shape
    depth = w_in.shape[0]
    h = x
    for i in range(depth):
        alpha = (2.0 * depth) ** 0.25
        qz, k, vt, g_attn, pool_mix = _in_proj_pool(
            h, w_in, w_pool, i, pool_scale[i][None, :])
        attn = _diff_attn(
            qz, k, vt, g_attn, lam_q1[i][None, :], lam_k1[i][None, :], lam_q2[i][None, :],
            lam_k2[i][None, :], subln_g[i][None, :], lam_init=_lambda_init(i))
        out = _out_proj(
            h.reshape(B * S, D), attn.reshape(B * S, ATTN_W), pool_mix.reshape(B * S, POOL_W),
            p[i].reshape(B * S, PLE_DIM), w_out, ln_g[i][None, :],
            ln_b[i][None, :], w_pe, w_pg, b_pg[i][None, :], i, alpha=alpha)
        h = out.reshape(B, S, D)
    return h
```

```python
import functools
import math

import jax
import jax.numpy as jnp
from jax import lax
from jax.experimental import pallas as pl
from jax.experimental.pallas import tpu as pltpu

D_MODEL = 1024
PLE_DIM = 256
N_HEADS = 4
HEAD_DIM = 64
HEAD_W = 2 * HEAD_DIM
ATTN_W = N_HEADS * HEAD_W
N_POOL_GROUPS = 4
POOL_WINDOWS = (2, 4, 8, 16)
POOL_CH = 128
POOL_W = N_POOL_GROUPS * POOL_CH
REST_W = ATTN_W + ATTN_W + 2 * POOL_W
IN_W = 2 * ATTN_W + REST_W
LN_EPS = 1e-5
LOG2_E = math.log2(math.e)
POOL_HALO = max(POOL_WINDOWS)

BF16_SUBLANE_TILE = 16
V_ROWS = HEAD_W + BF16_SUBLANE_TILE

SEQ_TILE = 512
KEY_TILE = 512
Q_SUBTILES = 2
OUT_TILE = 1024
OUT_CHUNK = 1024
V7X_VMEM_LIMIT_BYTES = 56 * 1024 * 1024

BF16 = jnp.bfloat16
F32 = jnp.float32
NT_DIMS = (((1,), (1,)), ((), ()))


def _lambda_init(layer_idx):
    return 0.8 - 0.6 * math.exp(-0.3 * layer_idx)


def _silu(g):
    return g / (1.0 + jnp.exp(-g))


def _in_proj_pool_kernel(x_ref, win_ref, wpool_ref, pscale_ref,
                         qz_ref, k_ref, vt_ref, gattn_ref, pool_ref,
                         ubuf_ref, wqv_ref, wrest_ref, *, tm, tk):
    i = pl.program_id(1)

    @pl.when((pl.program_id(0) == 0) & (i == 0))
    def _():
        wqv_ref[0:ATTN_W, :] = win_ref[:, 0:ATTN_W].T.astype(BF16)
        wqv_ref[ATTN_W:, :] = win_ref[:, 2 * ATTN_W:3 * ATTN_W].T.astype(BF16)
        wrest_ref[:, 0:ATTN_W] = win_ref[:, ATTN_W:2 * ATTN_W].astype(BF16)
        wrest_ref[:, ATTN_W:] = win_ref[:, 3 * ATTN_W:].astype(BF16)

    @pl.when(i == 0)
    def _():
        ubuf_ref[0:POOL_HALO, :] = jnp.zeros((POOL_HALO, POOL_W), F32)

    @pl.when(i > 0)
    def _():
        ubuf_ref[0:POOL_HALO, :] = ubuf_ref[tm:tm + POOL_HALO, :]

    xb = x_ref[...].astype(BF16)

    def proj(lo, width):
        return jnp.dot(xb, wrest_ref[:, lo:lo + width], preferred_element_type=F32)

    ubuf_ref[POOL_HALO:, :] = proj(2 * ATTN_W, POOL_W)
    pool_gate = pscale_ref[...] * _silu(proj(2 * ATTN_W + POOL_W, POOL_W))

    t = (i * tm + lax.broadcasted_iota(jnp.int32, (tm, 1), 0) + 1).astype(F32)
    deltas = []
    for g, w in enumerate(POOL_WINDOWS):
        ext = ubuf_ref[:, g * POOL_CH:(g + 1) * POOL_CH]
        acc = ext
        span = 1
        while span < w:
            acc = acc + pltpu.roll(acc, span, axis=0)
            span *= 2
        mean = acc[POOL_HALO:, :] / jnp.minimum(t, float(w))
        deltas.append((mean - ext[POOL_HALO:, :]).astype(BF16))

    scale = LOG2_E / math.sqrt(HEAD_DIM)
    q_t = lax.dot_general(wqv_ref[0:ATTN_W, :], xb, NT_DIMS, preferred_element_type=F32)
    v_t = lax.dot_general(wqv_ref[ATTN_W:, :], xb, NT_DIMS, preferred_element_type=F32)
    hq = tm // 2
    zeros = jnp.zeros((HEAD_DIM, hq), BF16)
    ones_tile = jnp.where(
        lax.broadcasted_iota(jnp.int32, (BF16_SUBLANE_TILE, tk), 0) == 0, 1.0, 0.0).astype(BF16)
    for h in range(N_HEADS):
        f0 = h * HEAD_W
        q1 = (q_t[f0:f0 + HEAD_DIM] * scale).astype(BF16)
        q2 = (q_t[f0 + HEAD_DIM:f0 + HEAD_W] * scale).astype(BF16)
        for half in range(2):
            qs = slice(half * hq, (half + 1) * hq)
            c1 = slice(half * tm, half * tm + hq)
            c2 = slice(half * tm + hq, (half + 1) * tm)
            qz_ref[f0:f0 + HEAD_DIM, c1] = q1[:, qs]
            qz_ref[f0 + HEAD_DIM:f0 + HEAD_W, c1] = zeros
            qz_ref[f0:f0 + HEAD_DIM, c2] = zeros
            qz_ref[f0 + HEAD_DIM:f0 + HEAD_W, c2] = q2[:, qs]
        r0 = h * V_ROWS
        for c in range(tm // tk):
            vt_ref[c, r0:r0 + HEAD_W, :] = v_t[f0:f0 + HEAD_W, c * tk:(c + 1) * tk].astype(BF16)
            vt_ref[c, r0 + HEAD_W:r0 + V_ROWS, :] = ones_tile

    k_ref[...] = proj(0, ATTN_W).astype(BF16)
    gattn_ref[...] = proj(ATTN_W, ATTN_W)

    for g in range(N_POOL_GROUPS):
        cols = slice(g * POOL_CH, (g + 1) * POOL_CH)
        y = jnp.dot(deltas[g], wpool_ref[g].astype(BF16), preferred_element_type=F32)
        pool_ref[:, cols] = (y * pool_gate[:, cols]).astype(BF16)


def _in_proj_pool(x, w_in, w_pool, layer, pool_scale):
    B, S, _ = x.shape
    tm, tk = SEQ_TILE, KEY_TILE
    nt = S // tm
    kern = functools.partial(_in_proj_pool_kernel, tm=tm, tk=tk)
    return pl.pallas_call(
        kern,
        grid=(B, nt),
        in_specs=[
            pl.BlockSpec((None, tm, D_MODEL), lambda b, i: (b, i, 0)),
            pl.BlockSpec((None, D_MODEL, IN_W), lambda b, i: (layer, 0, 0),
                         pipeline_mode=pl.Buffered(1)),
            pl.BlockSpec((None, N_POOL_GROUPS, POOL_CH, POOL_CH), lambda b, i: (layer, 0, 0, 0)),
            pl.BlockSpec((1, POOL_W), lambda b, i: (0, 0)),
        ],
        out_specs=[
            pl.BlockSpec((None, None, ATTN_W, 2 * tm), lambda b, i: (b, i, 0, 0)),
            pl.BlockSpec((None, tm, ATTN_W), lambda b, i: (b, i, 0)),
            pl.BlockSpec((None, tm // tk, N_HEADS * V_ROWS, tk), lambda b, i: (b, i, 0, 0)),
            pl.BlockSpec((None, tm, ATTN_W), lambda b, i: (b, i, 0)),
            pl.BlockSpec((None, tm, POOL_W), lambda b, i: (b, i, 0)),
        ],
        out_shape=[
            jax.ShapeDtypeStruct((B, nt, ATTN_W, 2 * tm), BF16),
            jax.ShapeDtypeStruct((B, S, ATTN_W), BF16),
            jax.ShapeDtypeStruct((B, S // tk, N_HEADS * V_ROWS, tk), BF16),
            jax.ShapeDtypeStruct((B, S, ATTN_W), F32),
            jax.ShapeDtypeStruct((B, S, POOL_W), BF16),
        ],
        scratch_shapes=[
            pltpu.VMEM((POOL_HALO + tm, POOL_W), F32),
            pltpu.VMEM((2 * ATTN_W, D_MODEL), BF16),
            pltpu.VMEM((D_MODEL, REST_W), BF16),
        ],
        compiler_params=pltpu.CompilerParams(
            dimension_semantics=("arbitrary", "arbitrary"),
            vmem_limit_bytes=V7X_VMEM_LIMIT_BYTES),
        name="in_proj_pool",
    )(x, w_in, w_pool, pool_scale)


def _diff_attn_kernel(lq1_ref, lk1_ref, lq2_ref, lk2_ref, subln_ref,
                      qz_ref, k_ref, vt_ref, g_ref, o_ref,
                      m_ref, acc_ref, s_ref, bias_ref, *, qt, tq, tk, lam_init):
    qi = pl.program_id(1)
    first_diag = qi * qt
    hq = tq // 2

    @pl.when((pl.program_id(0) == 0) & (qi == 0))
    def _():
        key = lax.broadcasted_iota(jnp.int32, (tk, tq), 0)
        col = lax.broadcasted_iota(jnp.int32, (tk, tq), 1)
        qry = hq + jnp.where(col >= hq, col - hq, col)
        bias_ref[0] = jnp.zeros((tk, tq), F32)
        bias_ref[1] = jnp.where(key <= qry, 0.0, -jnp.inf).astype(F32)

    m_ref[...] = jnp.full(m_ref.shape, -jnp.inf, F32)
    acc_ref[...] = jnp.zeros(acc_ref.shape, F32)

    def scores(j, unit, bias_slot, slot, first_keys_only=False):
        sub, h, half = unit
        qcols = slice(half * tq, (half + 1) * tq)
        nk = hq if first_keys_only else tk
        k_start = pl.multiple_of(j * tk, tk)
        kj = k_ref[pl.ds(k_start, nk), h * HEAD_W:(h + 1) * HEAD_W]
        s_t = jnp.dot(kj, qz_ref[sub, h * HEAD_W:(h + 1) * HEAD_W, qcols],
                      preferred_element_type=F32)
        if first_keys_only:
            s_t = s_t + bias_ref[1, hq:tk, :]
        elif bias_slot is not None:
            s_t = s_t + bias_ref[bias_slot]
        m_prev = m_ref[sub, h, :, qcols]
        m_new = jnp.maximum(m_prev, jnp.max(s_t, axis=0, keepdims=True))
        m_ref[sub, h, :, qcols] = m_new
        s_ref[slot, 0:nk, :] = s_t
        return m_prev, m_new

    def accumulate(j, unit, slot, m_prev, m_new, first_keys_only=False):
        sub, h, half = unit
        qcols = slice(half * tq, (half + 1) * tq)
        nk = hq if first_keys_only else tk
        vj = vt_ref[j, h * V_ROWS:(h + 1) * V_ROWS, 0:nk]
        alpha = jnp.exp2(m_prev - m_new)
        p_t = jnp.exp2(s_ref[slot, 0:nk, :] - m_new).astype(BF16)
        acc_ref[sub, h, :, qcols] = (alpha * acc_ref[sub, h, :, qcols]
                                     + jnp.dot(vj, p_t, preferred_element_type=F32))

    head_units = [(h, half) for h in range(N_HEADS) for half in (1, 0)]
    full_units = [(sub, h, half) for sub in range(qt) for (h, half) in head_units]
    first_unit = full_units[0]
    assert len(full_units) % 2 == 0

    def first_unit_bias_slot(j):
        return (j == first_diag).astype(jnp.int32)

    def body(j, pending):
        for idx, unit in enumerate(full_units):
            if idx + 1 < len(full_units):
                nxt = scores(j, full_units[idx + 1], None, (idx + 1) % 2)
            else:
                nxt = scores(j + 1, first_unit, first_unit_bias_slot(j + 1), 0)
            accumulate(j, unit, idx % 2, *pending)
            pending = nxt
        return pending

    pending = scores(0, first_unit, first_unit_bias_slot(0), 0)
    pending = lax.fori_loop(0, first_diag, body, pending)

    diag_items = []
    for d in range(qt):
        for sub in range(d, qt):
            for (h, half) in head_units:
                on_diag = sub == d
                diag_items.append((d, (sub, h, half),
                                   1 if on_diag and half == 1 else None,
                                   on_diag and half == 0))
    assert len(diag_items) % 2 == 0
    for idx, (d, unit, bias_slot, first_only) in enumerate(diag_items):
        if idx + 1 < len(diag_items):
            d2, unit2, bias2, first_only2 = diag_items[idx + 1]
            nxt = scores(first_diag + d2, unit2, bias2, (idx + 1) % 2, first_only2)
        else:
            nxt = None
        accumulate(first_diag + d, unit, idx % 2, *pending, first_keys_only=first_only)
        pending = nxt

    lam = (jnp.exp(jnp.sum(lq1_ref[...] * lk1_ref[...], axis=-1, keepdims=True))
           - jnp.exp(jnp.sum(lq2_ref[...] * lk2_ref[...], axis=-1, keepdims=True))
           + lam_init)
    for sub in range(qt):
        for h in range(N_HEADS):
            cols = slice(h * HEAD_W, (h + 1) * HEAD_W)
            acc = acc_ref[sub, h]
            inv_l = 1.0 / acc[HEAD_W:HEAD_W + 1, :]
            for half in range(2):
                rows = slice(sub * tq + half * hq, sub * tq + (half + 1) * hq)
                c1 = slice(half * tq, half * tq + hq)
                c2 = slice(half * tq + hq, (half + 1) * tq)
                o_t = (acc[0:HEAD_W, c1] * inv_l[:, c1]
                       - acc[0:HEAD_W, c2] * (lam * inv_l[:, c2]))
                ms = jnp.mean(o_t * o_t, axis=0, keepdims=True)
                o_t = o_t * (lax.rsqrt(ms + LN_EPS) * (1.0 - lam_init))
                o = o_t.T * subln_ref[...]
                o_ref[rows, cols] = (o * _silu(g_ref[rows, cols])).astype(BF16)


def _diff_attn(qz, k, vt, g_attn, lam_q1, lam_k1, lam_q2, lam_k2, subln_g, *, lam_init):
    B, S, _ = k.shape
    tq, tk, qt = SEQ_TILE, KEY_TILE, Q_SUBTILES
    assert tq == tk, "a diagonal block must be square for the shared causal bias tile"
    nt = S // tq
    kern = functools.partial(_diff_attn_kernel, qt=qt, tq=tq, tk=tk, lam_init=lam_init)
    vec = lambda n: pl.BlockSpec((1, n), lambda b, i: (0, 0))
    return pl.pallas_call(
        kern,
        grid=(B, nt // qt),
        in_specs=[
            vec(HEAD_DIM), vec(HEAD_DIM), vec(HEAD_DIM), vec(HEAD_DIM), vec(HEAD_W),
            pl.BlockSpec((None, qt, ATTN_W, 2 * tq), lambda b, i: (b, i, 0, 0)),
            pl.BlockSpec((None, S, ATTN_W), lambda b, i: (b, 0, 0)),
            pl.BlockSpec((None, S // tk, N_HEADS * V_ROWS, tk), lambda b, i: (b, 0, 0, 0)),
            pl.BlockSpec((None, qt * tq, ATTN_W), lambda b, i: (b, i, 0)),
        ],
        out_specs=pl.BlockSpec((None, qt * tq, ATTN_W), lambda b, i: (b, i, 0)),
        out_shape=jax.ShapeDtypeStruct((B, S, ATTN_W), BF16),
        scratch_shapes=[
            pltpu.VMEM((qt, N_HEADS, 1, 2 * tq), F32),
            pltpu.VMEM((qt, N_HEADS, V_ROWS, 2 * tq), F32),
            pltpu.VMEM((2, tk, tq), F32),
            pltpu.VMEM((2, tk, tq), F32),
        ],
        compiler_params=pltpu.CompilerParams(
            dimension_semantics=("arbitrary", "arbitrary"),
            vmem_limit_bytes=V7X_VMEM_LIMIT_BYTES),
        name="diff_attn",
    )(lam_q1, lam_k1, lam_q2, lam_k2, subln_g, qz, k, vt, g_attn)


def _out_proj_kernel(x_ref, attn_ref, pool_ref, p_ref, wout32_ref, lng_ref, lnb_ref,
                     wpe32_ref, wpg32_ref, bpg_ref, o_ref, wout_ref, wpe_ref, wpg_ref,
                     *, alpha, tm, chunk):
    @pl.when(pl.program_id(0) == 0)
    def _():
        wout_ref[...] = wout32_ref[...].astype(BF16)
        wpe_ref[...] = wpe32_ref[...].astype(BF16)
        wpg_ref[...] = wpg32_ref[...].astype(BF16)

    def branch_matmuls(c):
        rows = slice(c * chunk, (c + 1) * chunk)
        mix = (jnp.dot(attn_ref[rows, :], wout_ref[0:ATTN_W, :], preferred_element_type=F32)
               + jnp.dot(pool_ref[rows, :], wout_ref[ATTN_W:, :], preferred_element_type=F32))
        pe = jnp.dot(p_ref[rows, :].astype(BF16), wpe_ref[...], preferred_element_type=F32)
        return mix, pe

    def norm_and_gate(c, mix, pe):
        rows = slice(c * chunk, (c + 1) * chunk)
        r = alpha * x_ref[rows, :] + mix
        mu = jnp.mean(r, axis=-1, keepdims=True)
        cen = r - mu
        var = jnp.mean(cen * cen, axis=-1, keepdims=True)
        h = cen * lax.rsqrt(var + LN_EPS) * lng_ref[...] + lnb_ref[...]
        logits = jnp.dot(h.astype(BF16), wpg_ref[...], preferred_element_type=F32) + bpg_ref[...]
        gate = 1.0 / (1.0 + jnp.exp(-logits))
        o_ref[rows, :] = h + pe * gate

    pending = branch_matmuls(0)
    for c in range(tm // chunk):
        nxt = branch_matmuls(c + 1) if (c + 1) * chunk < tm else None
        norm_and_gate(c, *pending)
        pending = nxt


def _out_proj(x2, attn2, pool2, p2, w_out, ln_g, ln_b, w_pe, w_pg, b_pg, layer, *, alpha):
    R = x2.shape[0]
    tm, chunk = OUT_TILE, OUT_CHUNK
    kern = functools.partial(_out_proj_kernel, alpha=alpha, tm=tm, chunk=chunk)
    rows = lambda n: pl.BlockSpec((tm, n), lambda i: (i, 0))
    full = lambda a, b: pl.BlockSpec((a, b), lambda i: (0, 0))
    once = lambda a, b: pl.BlockSpec((None, a, b), lambda i: (layer, 0, 0),
                                     pipeline_mode=pl.Buffered(1))
    return pl.pallas_call(
        kern,
        grid=(R // tm,),
        in_specs=[
            rows(D_MODEL), rows(ATTN_W), rows(POOL_W), rows(PLE_DIM),
            once(ATTN_W + POOL_W, D_MODEL), full(1, D_MODEL), full(1, D_MODEL),
            once(PLE_DIM, D_MODEL), once(D_MODEL, D_MODEL), full(1, D_MODEL),
        ],
        out_specs=rows(D_MODEL),
        out_shape=jax.ShapeDtypeStruct((R, D_MODEL), F32),
        scratch_shapes=[
            pltpu.VMEM((ATTN_W + POOL_W, D_MODEL), BF16),
            pltpu.VMEM((PLE_DIM, D_MODEL), BF16),
            pltpu.VMEM((D_MODEL, D_MODEL), BF16),
        ],
        compiler_params=pltpu.CompilerParams(
            dimension_semantics=("arbitrary",),
            vmem_limit_bytes=V7X_VMEM_LIMIT_BYTES),
        name="out_proj_ln_ple",
    )(x2, attn2, pool2, p2, w_out, ln_g, ln_b, w_pe, w_pg, b_pg)


def kernel(x, p, w_in, lam_q1, lam_k1, lam_q2, lam_k2, subln_g, w_pool, pool_scale,
           w_out, ln_g, ln_b, w_pe, w_pg, b_pg):
    B, S, D = x.shape
    depth = w_in.shape[0]
    h = x
    for i in range(depth):
        alpha = (2.0 * depth) ** 0.25
        qz, k, vt, g_attn, pool_mix = _in_proj_pool(
            h, w_in, w_pool, i, pool_scale[i][None, :])
        attn = _diff_attn(
            qz, k, vt, g_attn, lam_q1[i][None, :], lam_k1[i][None, :], lam_q2[i][None, :],
            lam_k2[i][None, :], subln_g[i][None, :], lam_init=_lambda_init(i))
        out = _out_proj(
            h.reshape(B * S, D), attn.reshape(B * S, ATTN_W), pool_mix.reshape(B * S, POOL_W),
            p[i].reshape(B * S, PLE_DIM), w_out, ln_g[i][None, :],
            ln_b[i][None, :], w_pe, w_pg, b_pg[i][None, :], i, alpha=alpha)
        h = out.reshape(B, S, D)
    return h
```

```python
import functools
import math

import jax
import jax.numpy as jnp
from jax import lax
from jax.experimental import pallas as pl
from jax.experimental.pallas import tpu as pltpu

D_MODEL = 1024
PLE_DIM = 256
N_HEADS = 4
HEAD_DIM = 64
HEAD_W = 2 * HEAD_DIM
ATTN_W = N_HEADS * HEAD_W
N_POOL_GROUPS = 4
POOL_WINDOWS = (2, 4, 8, 16)
POOL_CH = 128
POOL_W = N_POOL_GROUPS * POOL_CH
REST_W = ATTN_W + ATTN_W + 2 * POOL_W
IN_W = 2 * ATTN_W + REST_W
LN_EPS = 1e-5
LOG2_E = math.log2(math.e)
POOL_HALO = max(POOL_WINDOWS)

BF16_SUBLANE_TILE = 16
V_ROWS = HEAD_W + BF16_SUBLANE_TILE

SEQ_TILE = 512
KEY_TILE = 512
Q_SUBTILES = 4
OUT_TILE = 1024
V7X_VMEM_LIMIT_BYTES = 56 * 1024 * 1024

BF16 = jnp.bfloat16
F32 = jnp.float32
NT_DIMS = (((1,), (1,)), ((), ()))


def _lambda_init(layer_idx):
    return 0.8 - 0.6 * math.exp(-0.3 * layer_idx)


def _silu(g):
    return g / (1.0 + jnp.exp(-g))


def _in_proj_pool_kernel(x_ref, win_ref, wpool_ref, pscale_ref,
                         qz_ref, k_ref, vt_ref, gattn_ref, pool_ref,
                         ubuf_ref, wqv_ref, wrest_ref, *, tm, tk):
    i = pl.program_id(1)

    @pl.when((pl.program_id(0) == 0) & (i == 0))
    def _():
        wqv_ref[0:ATTN_W, :] = win_ref[:, 0:ATTN_W].T.astype(BF16)
        wqv_ref[ATTN_W:, :] = win_ref[:, 2 * ATTN_W:3 * ATTN_W].T.astype(BF16)
        wrest_ref[:, 0:ATTN_W] = win_ref[:, ATTN_W:2 * ATTN_W].astype(BF16)
        wrest_ref[:, ATTN_W:] = win_ref[:, 3 * ATTN_W:].astype(BF16)

    @pl.when(i == 0)
    def _():
        ubuf_ref[0:POOL_HALO, :] = jnp.zeros((POOL_HALO, POOL_W), F32)

    @pl.when(i > 0)
    def _():
        ubuf_ref[0:POOL_HALO, :] = ubuf_ref[tm:tm + POOL_HALO, :]

    xb = x_ref[...].astype(BF16)

    def proj(lo, width):
        return jnp.dot(xb, wrest_ref[:, lo:lo + width], preferred_element_type=F32)

    ubuf_ref[POOL_HALO:, :] = proj(2 * ATTN_W, POOL_W)
    pool_gate = pscale_ref[...] * _silu(proj(2 * ATTN_W + POOL_W, POOL_W))

    t = (i * tm + lax.broadcasted_iota(jnp.int32, (tm, 1), 0) + 1).astype(F32)
    deltas = []
    for g, w in enumerate(POOL_WINDOWS):
        ext = ubuf_ref[:, g * POOL_CH:(g + 1) * POOL_CH]
        acc = ext
        span = 1
        while span < w:
            acc = acc + pltpu.roll(acc, span, axis=0)
            span *= 2
        mean = acc[POOL_HALO:, :] / jnp.minimum(t, float(w))
        deltas.append((mean - ext[POOL_HALO:, :]).astype(BF16))

    scale = LOG2_E / math.sqrt(HEAD_DIM)
    q_t = lax.dot_general(wqv_ref[0:ATTN_W, :], xb, NT_DIMS, preferred_element_type=F32)
    v_t = lax.dot_general(wqv_ref[ATTN_W:, :], xb, NT_DIMS, preferred_element_type=F32)
    hq = tm // 2
    zeros = jnp.zeros((HEAD_DIM, hq), BF16)
    ones_tile = jnp.where(
        lax.broadcasted_iota(jnp.int32, (BF16_SUBLANE_TILE, tk), 0) == 0, 1.0, 0.0).astype(BF16)
    for h in range(N_HEADS):
        f0 = h * HEAD_W
        q1 = (q_t[f0:f0 + HEAD_DIM] * scale).astype(BF16)
        q2 = (q_t[f0 + HEAD_DIM:f0 + HEAD_W] * scale).astype(BF16)
        for half in range(2):
            qs = slice(half * hq, (half + 1) * hq)
            c1 = slice(half * tm, half * tm + hq)
            c2 = slice(half * tm + hq, (half + 1) * tm)
            qz_ref[f0:f0 + HEAD_DIM, c1] = q1[:, qs]
            qz_ref[f0 + HEAD_DIM:f0 + HEAD_W, c1] = zeros
            qz_ref[f0:f0 + HEAD_DIM, c2] = zeros
            qz_ref[f0 + HEAD_DIM:f0 + HEAD_W, c2] = q2[:, qs]
        r0 = h * V_ROWS
        for c in range(tm // tk):
            vt_ref[c, r0:r0 + HEAD_W, :] = v_t[f0:f0 + HEAD_W, c * tk:(c + 1) * tk].astype(BF16)
            vt_ref[c, r0 + HEAD_W:r0 + V_ROWS, :] = ones_tile

    k_ref[...] = proj(0, ATTN_W).astype(BF16)
    gattn_ref[...] = proj(ATTN_W, ATTN_W)

    for g in range(N_POOL_GROUPS):
        cols = slice(g * POOL_CH, (g + 1) * POOL_CH)
        y = jnp.dot(deltas[g], wpool_ref[g].astype(BF16), preferred_element_type=F32)
        pool_ref[:, cols] = (y * pool_gate[:, cols]).astype(BF16)


def _in_proj_pool(x, w_in, w_pool, layer, pool_scale):
    B, S, _ = x.shape
    tm, tk = SEQ_TILE, KEY_TILE
    nt = S // tm
    kern = functools.partial(_in_proj_pool_kernel, tm=tm, tk=tk)
    return pl.pallas_call(
        kern,
        grid=(B, nt),
        in_specs=[
            pl.BlockSpec((None, tm, D_MODEL), lambda b, i: (b, i, 0)),
            pl.BlockSpec((None, D_MODEL, IN_W), lambda b, i: (layer, 0, 0),
                         pipeline_mode=pl.Buffered(1)),
            pl.BlockSpec((None, N_POOL_GROUPS, POOL_CH, POOL_CH), lambda b, i: (layer, 0, 0, 0)),
            pl.BlockSpec((1, POOL_W), lambda b, i: (0, 0)),
        ],
        out_specs=[
            pl.BlockSpec((None, None, ATTN_W, 2 * tm), lambda b, i: (b, i, 0, 0)),
            pl.BlockSpec((None, tm, ATTN_W), lambda b, i: (b, i, 0)),
            pl.BlockSpec((None, tm // tk, N_HEADS * V_ROWS, tk), lambda b, i: (b, i, 0, 0)),
            pl.BlockSpec((None, tm, ATTN_W), lambda b, i: (b, i, 0)),
            pl.BlockSpec((None, tm, POOL_W), lambda b, i: (b, i, 0)),
        ],
        out_shape=[
            jax.ShapeDtypeStruct((B, nt, ATTN_W, 2 * tm), BF16),
            jax.ShapeDtypeStruct((B, S, ATTN_W), BF16),
            jax.ShapeDtypeStruct((B, S // tk, N_HEADS * V_ROWS, tk), BF16),
            jax.ShapeDtypeStruct((B, S, ATTN_W), F32),
            jax.ShapeDtypeStruct((B, S, POOL_W), BF16),
        ],
        scratch_shapes=[
            pltpu.VMEM((POOL_HALO + tm, POOL_W), F32),
            pltpu.VMEM((2 * ATTN_W, D_MODEL), BF16),
            pltpu.VMEM((D_MODEL, REST_W), BF16),
        ],
        compiler_params=pltpu.CompilerParams(
            dimension_semantics=("arbitrary", "arbitrary"),
            vmem_limit_bytes=V7X_VMEM_LIMIT_BYTES),
        name="in_proj_pool",
    )(x, w_in, w_pool, pool_scale)


def _diff_attn_kernel(lq1_ref, lk1_ref, lq2_ref, lk2_ref, subln_ref,
                      qz_ref, k_ref, vt_ref, g_ref, o_ref,
                      m_ref, acc_ref, s_ref, bias_ref, *, qt, tq, tk, lam_init):
    qi = pl.program_id(1)
    first_diag = qi * qt
    hq = tq // 2

    @pl.when((pl.program_id(0) == 0) & (qi == 0))
    def _():
        key = lax.broadcasted_iota(jnp.int32, (tk, tq), 0)
        col = lax.broadcasted_iota(jnp.int32, (tk, tq), 1)
        qry = hq + jnp.where(col >= hq, col - hq, col)
        bias_ref[0] = jnp.zeros((tk, tq), F32)
        bias_ref[1] = jnp.where(key <= qry, 0.0, -jnp.inf).astype(F32)

    m_ref[...] = jnp.full(m_ref.shape, -jnp.inf, F32)
    acc_ref[...] = jnp.zeros(acc_ref.shape, F32)

    def scores(j, unit, bias_slot, slot, first_keys_only=False):
        sub, h, half = unit
        qcols = slice(half * tq, (half + 1) * tq)
        nk = hq if first_keys_only else tk
        k_start = pl.multiple_of(j * tk, tk)
        kj = k_ref[pl.ds(k_start, nk), h * HEAD_W:(h + 1) * HEAD_W]
        s_t = jnp.dot(kj, qz_ref[sub, h * HEAD_W:(h + 1) * HEAD_W, qcols],
                      preferred_element_type=F32)
        if first_keys_only:
            s_t = s_t + bias_ref[1, hq:tk, :]
        elif bias_slot is not None:
            s_t = s_t + bias_ref[bias_slot]
        m_prev = m_ref[sub, h, :, qcols]
        m_new = jnp.maximum(m_prev, jnp.max(s_t, axis=0, keepdims=True))
        m_ref[sub, h, :, qcols] = m_new
        s_ref[slot, 0:nk, :] = s_t
        return m_prev, m_new

    def accumulate(j, unit, slot, m_prev, m_new, first_keys_only=False):
        sub, h, half = unit
        qcols = slice(half * tq, (half + 1) * tq)
        nk = hq if first_keys_only else tk
        vj = vt_ref[j, h * V_ROWS:(h + 1) * V_ROWS, 0:nk]
        alpha = jnp.exp2(m_prev - m_new)
        p_t = jnp.exp2(s_ref[slot, 0:nk, :] - m_new).astype(BF16)
        acc_ref[sub, h, :, qcols] = (alpha * acc_ref[sub, h, :, qcols]
                                     + jnp.dot(vj, p_t, preferred_element_type=F32))

    head_units = [(h, half) for h in range(N_HEADS) for half in (1, 0)]
    full_units = [(sub, h, half) for sub in range(qt) for (h, half) in head_units]
    first_unit = full_units[0]
    assert len(full_units) % 2 == 0

    def first_unit_bias_slot(j):
        return (j == first_diag).astype(jnp.int32)

    def body(j, pending):
        for idx, unit in enumerate(full_units):
            if idx + 1 < len(full_units):
                nxt = scores(j, full_units[idx + 1], None, (idx + 1) % 2)
            else:
                nxt = scores(j + 1, first_unit, first_unit_bias_slot(j + 1), 0)
            accumulate(j, unit, idx % 2, *pending)
            pending = nxt
        return pending

    pending = scores(0, first_unit, first_unit_bias_slot(0), 0)
    pending = lax.fori_loop(0, first_diag, body, pending)

    diag_items = []
    for d in range(qt):
        for sub in range(d, qt):
            for (h, half) in head_units:
                on_diag = sub == d
                diag_items.append((d, (sub, h, half),
                                   1 if on_diag and half == 1 else None,
                                   on_diag and half == 0))
    assert len(diag_items) % 2 == 0
    for idx, (d, unit, bias_slot, first_only) in enumerate(diag_items):
        if idx + 1 < len(diag_items):
            d2, unit2, bias2, first_only2 = diag_items[idx + 1]
            nxt = scores(first_diag + d2, unit2, bias2, (idx + 1) % 2, first_only2)
        else:
            nxt = None
        accumulate(first_diag + d, unit, idx % 2, *pending, first_keys_only=first_only)
        pending = nxt

    lam = (jnp.exp(jnp.sum(lq1_ref[...] * lk1_ref[...], axis=-1, keepdims=True))
           - jnp.exp(jnp.sum(lq2_ref[...] * lk2_ref[...], axis=-1, keepdims=True))
           + lam_init)
    for sub in range(qt):
        for h in range(N_HEADS):
            cols = slice(h * HEAD_W, (h + 1) * HEAD_W)
            acc = acc_ref[sub, h]
            inv_l = 1.0 / acc[HEAD_W:HEAD_W + 1, :]
            for half in range(2):
                rows = slice(sub * tq + half * hq, sub * tq + (half + 1) * hq)
                c1 = slice(half * tq, half * tq + hq)
                c2 = slice(half * tq + hq, (half + 1) * tq)
                o_t = (acc[0:HEAD_W, c1] * inv_l[:, c1]
                       - acc[0:HEAD_W, c2] * (lam * inv_l[:, c2]))
                ms = jnp.mean(o_t * o_t, axis=0, keepdims=True)
                o_t = o_t * (lax.rsqrt(ms + LN_EPS) * (1.0 - lam_init))
                o = o_t.T * subln_ref[...]
                o_ref[rows, cols] = (o * _silu(g_ref[rows, cols])).astype(BF16)


def _diff_attn(qz, k, vt, g_attn, lam_q1, lam_k1, lam_q2, lam_k2, subln_g, *, lam_init):
    B, S, _ = k.shape
    tq, tk, qt = SEQ_TILE, KEY_TILE, Q_SUBTILES
    assert tq == tk, "a diagonal block must be square for the shared causal bias tile"
    nt = S // tq
    kern = functools.partial(_diff_attn_kernel, qt=qt, tq=tq, tk=tk, lam_init=lam_init)
    vec = lambda n: pl.BlockSpec((1, n), lambda b, i: (0, 0))
    return pl.pallas_call(
        kern,
        grid=(B, nt // qt),
        in_specs=[
            vec(HEAD_DIM), vec(HEAD_DIM), vec(HEAD_DIM), vec(HEAD_DIM), vec(HEAD_W),
            pl.BlockSpec((None, qt, ATTN_W, 2 * tq), lambda b, i: (b, i, 0, 0)),
            pl.BlockSpec((None, S, ATTN_W), lambda b, i: (b, 0, 0)),
            pl.BlockSpec((None, S // tk, N_HEADS * V_ROWS, tk), lambda b, i: (b, 0, 0, 0)),
            pl.BlockSpec((None, qt * tq, ATTN_W), lambda b, i: (b, i, 0)),
        ],
        out_specs=pl.BlockSpec((None, qt * tq, ATTN_W), lambda b, i: (b, i, 0)),
        out_shape=jax.ShapeDtypeStruct((B, S, ATTN_W), BF16),
        scratch_shapes=[
            pltpu.VMEM((qt, N_HEADS, 1, 2 * tq), F32),
            pltpu.VMEM((qt, N_HEADS, V_ROWS, 2 * tq), F32),
            pltpu.VMEM((2, tk, tq), F32),
            pltpu.VMEM((2, tk, tq), F32),
        ],
        compiler_params=pltpu.CompilerParams(
            dimension_semantics=("arbitrary", "arbitrary"),
            vmem_limit_bytes=V7X_VMEM_LIMIT_BYTES),
        name="diff_attn",
    )(lam_q1, lam_k1, lam_q2, lam_k2, subln_g, qz, k, vt, g_attn)


def _out_proj_kernel(x_ref, attn_ref, pool_ref, p_ref, wout32_ref, lng_ref, lnb_ref,
                     wpe32_ref, wpg32_ref, bpg_ref, o_ref, wout_ref, wpe_ref, wpg_ref,
                     *, alpha):
    @pl.when(pl.program_id(0) == 0)
    def _():
        wout_ref[...] = wout32_ref[...].astype(BF16)
        wpe_ref[...] = wpe32_ref[...].astype(BF16)
        wpg_ref[...] = wpg32_ref[...].astype(BF16)

    mix = (jnp.dot(attn_ref[...], wout_ref[0:ATTN_W, :], preferred_element_type=F32)
           + jnp.dot(pool_ref[...], wout_ref[ATTN_W:, :], preferred_element_type=F32))
    pe = jnp.dot(p_ref[...].astype(BF16), wpe_ref[...], preferred_element_type=F32)
    r = alpha * x_ref[...] + mix
    mu = jnp.mean(r, axis=-1, keepdims=True)
    cen = r - mu
    var = jnp.mean(cen * cen, axis=-1, keepdims=True)
    h = cen * lax.rsqrt(var + LN_EPS) * lng_ref[...] + lnb_ref[...]
    logits = jnp.dot(h.astype(BF16), wpg_ref[...], preferred_element_type=F32) + bpg_ref[...]
    gate = 1.0 / (1.0 + jnp.exp(-logits))
    o_ref[...] = h + pe * gate


def _out_proj(x2, attn2, pool2, p2, w_out, ln_g, ln_b, w_pe, w_pg, b_pg, layer, *, alpha):
    R = x2.shape[0]
    tm = OUT_TILE
    kern = functools.partial(_out_proj_kernel, alpha=alpha)
    rows = lambda n: pl.BlockSpec((tm, n), lambda i: (i, 0))
    full = lambda a, b: pl.BlockSpec((a, b), lambda i: (0, 0))
    once = lambda a, b: pl.BlockSpec((None, a, b), lambda i: (layer, 0, 0),
                                     pipeline_mode=pl.Buffered(1))
    return pl.pallas_call(
        kern,
        grid=(R // tm,),
        in_specs=[
            rows(D_MODEL), rows(ATTN_W), rows(POOL_W), rows(PLE_DIM),
            once(ATTN_W + POOL_W, D_MODEL), full(1, D_MODEL), full(1, D_MODEL),
            once(PLE_DIM, D_MODEL), once(D_MODEL, D_MODEL), full(1, D_MODEL),
        ],
        out_specs=rows(D_MODEL),
        out_shape=jax.ShapeDtypeStruct((R, D_MODEL), F32),
        scratch_shapes=[
            pltpu.VMEM((ATTN_W + POOL_W, D_MODEL), BF16),
            pltpu.VMEM((PLE_DIM, D_MODEL), BF16),
            pltpu.VMEM((D_MODEL, D_MODEL), BF16),
        ],
        compiler_params=pltpu.CompilerParams(
            dimension_semantics=("arbitrary",),
            vmem_limit_bytes=V7X_VMEM_LIMIT_BYTES),
        name="out_proj_ln_ple",
    )(x2, attn2, pool2, p2, w_out, ln_g, ln_b, w_pe, w_pg, b_pg)


def kernel(x, p, w_in, lam_q1, lam_k1, lam_q2, lam_k2, subln_g, w_pool, pool_scale,
           w_out, ln_g, ln_b, w_pe, w_pg, b_pg):
    B, S, D = x.shape
    depth = w_in.shape[0]
    h = x
    for i in range(depth):
        alpha = (2.0 * depth) ** 0.25
        qz, k, vt, g_attn, pool_mix = _in_proj_pool(
            h, w_in, w_pool, i, pool_scale[i][None, :])
        attn = _diff_attn(
            qz, k, vt, g_attn, lam_q1[i][None, :], lam_k1[i][None, :], lam_q2[i][None, :],
            lam_k2[i][None, :], subln_g[i][None, :], lam_init=_lambda_init(i))
        out = _out_proj(
            h.reshape(B * S, D), attn.reshape(B * S, ATTN_W), pool_mix.reshape(B * S, POOL_W),
            p[i].reshape(B * S, PLE_DIM), w_out, ln_g[i][None, :],
            ln_b[i][None, :], w_pe, w_pg, b_pg[i][None, :], i, alpha=alpha)
        h = out.reshape(B, S, D)
    return h
```

```python
import functools
import math

import jax
import jax.numpy as jnp
from jax import lax
from jax.experimental import pallas as pl
from jax.experimental.pallas import tpu as pltpu

D_MODEL = 1024
PLE_DIM = 256
N_HEADS = 4
HEAD_DIM = 64
HEAD_W = 2 * HEAD_DIM
ATTN_W = N_HEADS * HEAD_W
N_POOL_GROUPS = 4
POOL_WINDOWS = (2, 4, 8, 16)
POOL_CH = 128
POOL_W = N_POOL_GROUPS * POOL_CH
REST_W = ATTN_W + ATTN_W + 2 * POOL_W
IN_W = 2 * ATTN_W + REST_W
LN_EPS = 1e-5
LOG2_E = math.log2(math.e)
POOL_HALO = max(POOL_WINDOWS)

BF16_SUBLANE_TILE = 16
V_ROWS = HEAD_W + BF16_SUBLANE_TILE

SEQ_TILE = 512
KEY_TILE = 512
Q_SUBTILES = 2
OUT_TILE = 1024
V7X_VMEM_LIMIT_BYTES = 56 * 1024 * 1024

BF16 = jnp.bfloat16
F32 = jnp.float32
NT_DIMS = (((1,), (1,)), ((), ()))


def _lambda_init(layer_idx):
    return 0.8 - 0.6 * math.exp(-0.3 * layer_idx)


def _silu(g):
    return g / (1.0 + jnp.exp(-g))


def _in_proj_pool_kernel(x_ref, win_ref, wpool_ref, pscale_ref,
                         qz_ref, k_ref, vt_ref, gattn_ref, pool_ref,
                         ubuf_ref, wqv_ref, wrest_ref, *, tm, tk):
    i = pl.program_id(1)

    @pl.when((pl.program_id(0) == 0) & (i == 0))
    def _():
        wqv_ref[0:ATTN_W, :] = win_ref[:, 0:ATTN_W].T.astype(BF16)
        wqv_ref[ATTN_W:, :] = win_ref[:, 2 * ATTN_W:3 * ATTN_W].T.astype(BF16)
        wrest_ref[:, 0:ATTN_W] = win_ref[:, ATTN_W:2 * ATTN_W].astype(BF16)
        wrest_ref[:, ATTN_W:] = win_ref[:, 3 * ATTN_W:].astype(BF16)

    @pl.when(i == 0)
    def _():
        ubuf_ref[0:POOL_HALO, :] = jnp.zeros((POOL_HALO, POOL_W), F32)

    @pl.when(i > 0)
    def _():
        ubuf_ref[0:POOL_HALO, :] = ubuf_ref[tm:tm + POOL_HALO, :]

    xb = x_ref[...].astype(BF16)

    def proj(lo, width):
        return jnp.dot(xb, wrest_ref[:, lo:lo + width], preferred_element_type=F32)

    ubuf_ref[POOL_HALO:, :] = proj(2 * ATTN_W, POOL_W)
    pool_gate = pscale_ref[...] * _silu(proj(2 * ATTN_W + POOL_W, POOL_W))

    t = (i * tm + lax.broadcasted_iota(jnp.int32, (tm, 1), 0) + 1).astype(F32)
    deltas = []
    for g, w in enumerate(POOL_WINDOWS):
        ext = ubuf_ref[:, g * POOL_CH:(g + 1) * POOL_CH]
        acc = ext
        span = 1
        while span < w:
            acc = acc + pltpu.roll(acc, span, axis=0)
            span *= 2
        mean = acc[POOL_HALO:, :] / jnp.minimum(t, float(w))
        deltas.append((mean - ext[POOL_HALO:, :]).astype(BF16))

    scale = LOG2_E / math.sqrt(HEAD_DIM)
    q_t = lax.dot_general(wqv_ref[0:ATTN_W, :], xb, NT_DIMS, preferred_element_type=F32)
    v_t = lax.dot_general(wqv_ref[ATTN_W:, :], xb, NT_DIMS, preferred_element_type=F32)
    hq = tm // 2
    zeros = jnp.zeros((HEAD_DIM, hq), BF16)
    ones_tile = jnp.where(
        lax.broadcasted_iota(jnp.int32, (BF16_SUBLANE_TILE, tk), 0) == 0, 1.0, 0.0).astype(BF16)
    for h in range(N_HEADS):
        f0 = h * HEAD_W
        q1 = (q_t[f0:f0 + HEAD_DIM] * scale).astype(BF16)
        q2 = (q_t[f0 + HEAD_DIM:f0 + HEAD_W] * scale).astype(BF16)
        for half in range(2):
            qs = slice(half * hq, (half + 1) * hq)
            c1 = slice(half * tm, half * tm + hq)
            c2 = slice(half * tm + hq, (half + 1) * tm)
            qz_ref[f0:f0 + HEAD_DIM, c1] = q1[:, qs]
            qz_ref[f0 + HEAD_DIM:f0 + HEAD_W, c1] = zeros
            qz_ref[f0:f0 + HEAD_DIM, c2] = zeros
            qz_ref[f0 + HEAD_DIM:f0 + HEAD_W, c2] = q2[:, qs]
        r0 = h * V_ROWS
        for c in range(tm // tk):
            vt_ref[c, r0:r0 + HEAD_W, :] = v_t[f0:f0 + HEAD_W, c * tk:(c + 1) * tk].astype(BF16)
            vt_ref[c, r0 + HEAD_W:r0 + V_ROWS, :] = ones_tile

    k_ref[...] = proj(0, ATTN_W).astype(BF16)
    gattn_ref[...] = proj(ATTN_W, ATTN_W)

    for g in range(N_POOL_GROUPS):
        cols = slice(g * POOL_CH, (g + 1) * POOL_CH)
        y = jnp.dot(deltas[g], wpool_ref[g].astype(BF16), preferred_element_type=F32)
        pool_ref[:, cols] = (y * pool_gate[:, cols]).astype(BF16)


def _in_proj_pool(x, w_in, w_pool, layer, pool_scale):
    B, S, _ = x.shape
    tm, tk = SEQ_TILE, KEY_TILE
    nt = S // tm
    kern = functools.partial(_in_proj_pool_kernel, tm=tm, tk=tk)
    return pl.pallas_call(
        kern,
        grid=(B, nt),
        in_specs=[
            pl.BlockSpec((None, tm, D_MODEL), lambda b, i: (b, i, 0)),
            pl.BlockSpec((None, D_MODEL, IN_W), lambda b, i: (layer, 0, 0),
                         pipeline_mode=pl.Buffered(1)),
            pl.BlockSpec((None, N_POOL_GROUPS, POOL_CH, POOL_CH), lambda b, i: (layer, 0, 0, 0)),
            pl.BlockSpec((1, POOL_W), lambda b, i: (0, 0)),
        ],
        out_specs=[
            pl.BlockSpec((None, None, ATTN_W, 2 * tm), lambda b, i: (b, i, 0, 0)),
            pl.BlockSpec((None, tm, ATTN_W), lambda b, i: (b, i, 0)),
            pl.BlockSpec((None, tm // tk, N_HEADS * V_ROWS, tk), lambda b, i: (b, i, 0, 0)),
            pl.BlockSpec((None, tm, ATTN_W), lambda b, i: (b, i, 0)),
            pl.BlockSpec((None, tm, POOL_W), lambda b, i: (b, i, 0)),
        ],
        out_shape=[
            jax.ShapeDtypeStruct((B, nt, ATTN_W, 2 * tm), BF16),
            jax.ShapeDtypeStruct((B, S, ATTN_W), BF16),
            jax.ShapeDtypeStruct((B, S // tk, N_HEADS * V_ROWS, tk), BF16),
            jax.ShapeDtypeStruct((B, S, ATTN_W), F32),
            jax.ShapeDtypeStruct((B, S, POOL_W), BF16),
        ],
        scratch_shapes=[
            pltpu.VMEM((POOL_HALO + tm, POOL_W), F32),
            pltpu.VMEM((2 * ATTN_W, D_MODEL), BF16),
            pltpu.VMEM((D_MODEL, REST_W), BF16),
        ],
        compiler_params=pltpu.CompilerParams(
            dimension_semantics=("arbitrary", "arbitrary"),
            vmem_limit_bytes=V7X_VMEM_LIMIT_BYTES),
        name="in_proj_pool",
    )(x, w_in, w_pool, pool_scale)


def _diff_attn_kernel(lq1_ref, lk1_ref, lq2_ref, lk2_ref, subln_ref,
                      qz_ref, k_ref, vt_ref, g_ref, o_ref,
                      m_ref, acc_ref, s_ref, bias_ref, *, qt, tq, tk, lam_init):
    qi = pl.program_id(1)
    first_diag = qi * qt
    hq = tq // 2

    @pl.when((pl.program_id(0) == 0) & (qi == 0))
    def _():
        key = lax.broadcasted_iota(jnp.int32, (tk, tq), 0)
        col = lax.broadcasted_iota(jnp.int32, (tk, tq), 1)
        qry = hq + jnp.where(col >= hq, col - hq, col)
        bias_ref[0] = jnp.zeros((tk, tq), F32)
        bias_ref[1] = jnp.where(key <= qry, 0.0, -jnp.inf).astype(F32)

    m_ref[...] = jnp.full(m_ref.shape, -jnp.inf, F32)
    acc_ref[...] = jnp.zeros(acc_ref.shape, F32)

    def scores(j, unit, bias_slot, slot, first_keys_only=False):
        sub, h, half = unit
        qcols = slice(half * tq, (half + 1) * tq)
        nk = hq if first_keys_only else tk
        k_start = pl.multiple_of(j * tk, tk)
        kj = k_ref[pl.ds(k_start, nk), h * HEAD_W:(h + 1) * HEAD_W]
        s_t = jnp.dot(kj, qz_ref[sub, h * HEAD_W:(h + 1) * HEAD_W, qcols],
                      preferred_element_type=F32)
        if first_keys_only:
            s_t = s_t + bias_ref[1, hq:tk, :]
        elif bias_slot is not None:
            s_t = s_t + bias_ref[bias_slot]
        m_prev = m_ref[sub, h, :, qcols]
        m_new = jnp.maximum(m_prev, jnp.max(s_t, axis=0, keepdims=True))
        m_ref[sub, h, :, qcols] = m_new
        s_ref[slot, 0:nk, :] = s_t
        return m_prev, m_new

    def accumulate(j, unit, slot, m_prev, m_new, first_keys_only=False):
        sub, h, half = unit
        qcols = slice(half * tq, (half + 1) * tq)
        nk = hq if first_keys_only else tk
        vj = vt_ref[j, h * V_ROWS:(h + 1) * V_ROWS, 0:nk]
        alpha = jnp.exp2(m_prev - m_new)
        p_t = jnp.exp2(s_ref[slot, 0:nk, :] - m_new).astype(BF16)
        acc_ref[sub, h, :, qcols] = (alpha * acc_ref[sub, h, :, qcols]
                                     + jnp.dot(vj, p_t, preferred_element_type=F32))

    head_units = [(h, half) for h in range(N_HEADS) for half in (1, 0)]
    full_units = [(sub, h, half) for sub in range(qt) for (h, half) in head_units]
    first_unit = full_units[0]
    assert len(full_units) % 2 == 0

    def first_unit_bias_slot(j):
        return (j == first_diag).astype(jnp.int32)

    def body(j, pending):
        for idx, unit in enumerate(full_units):
            if idx + 1 < len(full_units):
                nxt = scores(j, full_units[idx + 1], None, (idx + 1) % 2)
            else:
                nxt = scores(j + 1, first_unit, first_unit_bias_slot(j + 1), 0)
            accumulate(j, unit, idx % 2, *pending)
            pending = nxt
        return pending

    pending = scores(0, first_unit, first_unit_bias_slot(0), 0)
    pending = lax.fori_loop(0, first_diag, body, pending)

    diag_items = []
    for d in range(qt):
        for sub in range(d, qt):
            for (h, half) in head_units:
                on_diag = sub == d
                diag_items.append((d, (sub, h, half),
                                   1 if on_diag and half == 1 else None,
                                   on_diag and half == 0))
    assert len(diag_items) % 2 == 0
    for idx, (d, unit, bias_slot, first_only) in enumerate(diag_items):
        if idx + 1 < len(diag_items):
            d2, unit2, bias2, first_only2 = diag_items[idx + 1]
            nxt = scores(first_diag + d2, unit2, bias2, (idx + 1) % 2, first_only2)
        else:
            nxt = None
        accumulate(first_diag + d, unit, idx % 2, *pending, first_keys_only=first_only)
        pending = nxt

    lam = (jnp.exp(jnp.sum(lq1_ref[...] * lk1_ref[...], axis=-1, keepdims=True))
           - jnp.exp(jnp.sum(lq2_ref[...] * lk2_ref[...], axis=-1, keepdims=True))
           + lam_init)
    for sub in range(qt):
        for h in range(N_HEADS):
            cols = slice(h * HEAD_W, (h + 1) * HEAD_W)
            acc = acc_ref[sub, h]
            inv_l = 1.0 / acc[HEAD_W:HEAD_W + 1, :]
            for half in range(2):
                rows = slice(sub * tq + half * hq, sub * tq + (half + 1) * hq)
                c1 = slice(half * tq, half * tq + hq)
                c2 = slice(half * tq + hq, (half + 1) * tq)
                o_t = (acc[0:HEAD_W, c1] * inv_l[:, c1]
                       - acc[0:HEAD_W, c2] * (lam * inv_l[:, c2]))
                ms = jnp.mean(o_t * o_t, axis=0, keepdims=True)
                o_t = o_t * (lax.rsqrt(ms + LN_EPS) * (1.0 - lam_init))
                o = o_t.T * subln_ref[...]
                o_ref[rows, cols] = (o * _silu(g_ref[rows, cols])).astype(BF16)


def _diff_attn(qz, k, vt, g_attn, lam_q1, lam_k1, lam_q2, lam_k2, subln_g, *, lam_init):
    B, S, _ = k.shape
    tq, tk, qt = SEQ_TILE, KEY_TILE, Q_SUBTILES
    assert tq == tk, "a diagonal block must be square for the shared causal bias tile"
    nt = S // tq
    kern = functools.partial(_diff_attn_kernel, qt=qt, tq=tq, tk=tk, lam_init=lam_init)
    vec = lambda n: pl.BlockSpec((1, n), lambda b, i: (0, 0))
    return pl.pallas_call(
        kern,
        grid=(B, nt // qt),
        in_specs=[
            vec(HEAD_DIM), vec(HEAD_DIM), vec(HEAD_DIM), vec(HEAD_DIM), vec(HEAD_W),
            pl.BlockSpec((None, qt, ATTN_W, 2 * tq), lambda b, i: (b, i, 0, 0)),
            pl.BlockSpec((None, S, ATTN_W), lambda b, i: (b, 0, 0)),
            pl.BlockSpec((None, S // tk, N_HEADS * V_ROWS, tk), lambda b, i: (b, 0, 0, 0)),
            pl.BlockSpec((None, qt * tq, ATTN_W), lambda b, i: (b, i, 0)),
        ],
        out_specs=pl.BlockSpec((None, qt * tq, ATTN_W), lambda b, i: (b, i, 0)),
        out_shape=jax.ShapeDtypeStruct((B, S, ATTN_W), BF16),
        scratch_shapes=[
            pltpu.VMEM((qt, N_HEADS, 1, 2 * tq), F32),
            pltpu.VMEM((qt, N_HEADS, V_ROWS, 2 * tq), F32),
            pltpu.VMEM((2, tk, tq), F32),
            pltpu.VMEM((2, tk, tq), F32),
        ],
        compiler_params=pltpu.CompilerParams(
            dimension_semantics=("arbitrary", "arbitrary"),
            vmem_limit_bytes=V7X_VMEM_LIMIT_BYTES),
        name="diff_attn",
    )(lam_q1, lam_k1, lam_q2, lam_k2, subln_g, qz, k, vt, g_attn)


def _out_proj_kernel(x_ref, attn_ref, pool_ref, p_ref, wout32_ref, lng_ref, lnb_ref,
                     wpe32_ref, wpg32_ref, bpg_ref, o_ref, wout_ref, wpe_ref, wpg_ref,
                     *, alpha):
    @pl.when(pl.program_id(0) == 0)
    def _():
        wout_ref[...] = wout32_ref[...].astype(BF16)
        wpe_ref[...] = wpe32_ref[...].astype(BF16)
        wpg_ref[...] = wpg32_ref[...].astype(BF16)

    mix = (jnp.dot(attn_ref[...], wout_ref[0:ATTN_W, :], preferred_element_type=F32)
           + jnp.dot(pool_ref[...], wout_ref[ATTN_W:, :], preferred_element_type=F32))
    pe = jnp.dot(p_ref[...].astype(BF16), wpe_ref[...], preferred_element_type=F32)
    r = alpha * x_ref[...] + mix
    mu = jnp.mean(r, axis=-1, keepdims=True)
    cen = r - mu
    var = jnp.mean(cen * cen, axis=-1, keepdims=True)
    h = cen * lax.rsqrt(var + LN_EPS) * lng_ref[...] + lnb_ref[...]
    logits = jnp.dot(h.astype(BF16), wpg_ref[...], preferred_element_type=F32) + bpg_ref[...]
    gate = 1.0 / (1.0 + jnp.exp(-logits))
    o_ref[...] = h + pe * gate


def _out_proj(x2, attn2, pool2, p2, w_out, ln_g, ln_b, w_pe, w_pg, b_pg, layer, *, alpha):
    R = x2.shape[0]
    tm = OUT_TILE
    kern = functools.partial(_out_proj_kernel, alpha=alpha)
    rows = lambda n: pl.BlockSpec((tm, n), lambda i: (i, 0))
    full = lambda a, b: pl.BlockSpec((a, b), lambda i: (0, 0))
    once = lambda a, b: pl.BlockSpec((None, a, b), lambda i: (layer, 0, 0),
                                     pipeline_mode=pl.Buffered(1))
    return pl.pallas_call(
        kern,
        grid=(R // tm,),
        in_specs=[
            rows(D_MODEL), rows(ATTN_W), rows(POOL_W), rows(PLE_DIM),
            once(ATTN_W + POOL_W, D_MODEL), full(1, D_MODEL), full(1, D_MODEL),
            once(PLE_DIM, D_MODEL), once(D_MODEL, D_MODEL), full(1, D_MODEL),
        ],
        out_specs=rows(D_MODEL),
        out_shape=jax.ShapeDtypeStruct((R, D_MODEL), F32),
        scratch_shapes=[
            pltpu.VMEM((ATTN_W + POOL_W, D_MODEL), BF16),
            pltpu.VMEM((PLE_DIM, D_MODEL), BF16),
            pltpu.VMEM((D_MODEL, D_MODEL), BF16),
        ],
        compiler_params=pltpu.CompilerParams(
            dimension_semantics=("arbitrary",),
            vmem_limit_bytes=V7X_VMEM_LIMIT_BYTES),
        name="out_proj_ln_ple",
    )(x2, attn2, pool2, p2, w_out, ln_g, ln_b, w_pe, w_pg, b_pg)


def kernel(x, p, w_in, lam_q1, lam_k1, lam_q2, lam_k2, subln_g, w_pool, pool_scale,
           w_out, ln_g, ln_b, w_pe, w_pg, b_pg):
    B, S, D = x.shape
    depth = w_in.shape[0]
    h = x
    for i in range(depth):
        alpha = (2.0 * depth) ** 0.25
        qz, k, vt, g_attn, pool_mix = _in_proj_pool(
            h, w_in, w_pool, i, pool_scale[i][None, :])
        attn = _diff_attn(
            qz, k, vt, g_attn, lam_q1[i][None, :], lam_k1[i][None, :], lam_q2[i][None, :],
            lam_k2[i][None, :], subln_g[i][None, :], lam_init=_lambda_init(i))
        out = _out_proj(
            h.reshape(B * S, D), attn.reshape(B * S, ATTN_W), pool_mix.reshape(B * S, POOL_W),
            p[i].reshape(B * S, PLE_DIM), w_out, ln_g[i][None, :],
            ln_b[i][None, :], w_pe, w_pg, b_pg[i][None, :], i, alpha=alpha)
        h = out.reshape(B, S, D)
    return h
```

```python
import functools
import math

import jax
import jax.numpy as jnp
from jax import lax
from jax.experimental import pallas as pl
from jax.experimental.pallas import tpu as pltpu

D_MODEL = 1024
PLE_DIM = 256
N_HEADS = 4
HEAD_DIM = 64
HEAD_W = 2 * HEAD_DIM
ATTN_W = N_HEADS * HEAD_W
N_POOL_GROUPS = 4
POOL_WINDOWS = (2, 4, 8, 16)
POOL_CH = 128
POOL_W = N_POOL_GROUPS * POOL_CH
REST_W = ATTN_W + ATTN_W + 2 * POOL_W
IN_W = 2 * ATTN_W + REST_W
LN_EPS = 1e-5
LOG2_E = math.log2(math.e)
POOL_HALO = max(POOL_WINDOWS)

BF16_SUBLANE_TILE = 16
V_ROWS = HEAD_W + BF16_SUBLANE_TILE

SEQ_TILE = 512
KEY_TILE = 512
Q_SUBTILES = 2
OUT_TILE = 1024
V7X_VMEM_LIMIT_BYTES = 56 * 1024 * 1024

BF16 = jnp.bfloat16
F32 = jnp.float32
NT_DIMS = (((1,), (1,)), ((), ()))


def _lambda_init(layer_idx):
    return 0.8 - 0.6 * math.exp(-0.3 * layer_idx)


def _silu(g):
    return g / (1.0 + jnp.exp(-g))


def _in_proj_pool_kernel(x_ref, win_ref, wpool_ref, pscale_ref,
                         qz_ref, k_ref, vt_ref, gattn_ref, pool_ref,
                         ubuf_ref, wqv_ref, wrest_ref, *, tm, tk):
    i = pl.program_id(1)

    @pl.when((pl.program_id(0) == 0) & (i == 0))
    def _():
        wqv_ref[0:ATTN_W, :] = win_ref[:, 0:ATTN_W].T.astype(BF16)
        wqv_ref[ATTN_W:, :] = win_ref[:, 2 * ATTN_W:3 * ATTN_W].T.astype(BF16)
        wrest_ref[:, 0:ATTN_W] = win_ref[:, ATTN_W:2 * ATTN_W].astype(BF16)
        wrest_ref[:, ATTN_W:] = win_ref[:, 3 * ATTN_W:].astype(BF16)

    @pl.when(i == 0)
    def _():
        ubuf_ref[0:POOL_HALO, :] = jnp.zeros((POOL_HALO, POOL_W), F32)

    @pl.when(i > 0)
    def _():
        ubuf_ref[0:POOL_HALO, :] = ubuf_ref[tm:tm + POOL_HALO, :]

    xb = x_ref[...].astype(BF16)

    def proj(lo, width):
        return jnp.dot(xb, wrest_ref[:, lo:lo + width], preferred_element_type=F32)

    ubuf_ref[POOL_HALO:, :] = proj(2 * ATTN_W, POOL_W)
    pool_gate = pscale_ref[...] * _silu(proj(2 * ATTN_W + POOL_W, POOL_W))

    t = (i * tm + lax.broadcasted_iota(jnp.int32, (tm, 1), 0) + 1).astype(F32)
    deltas = []
    for g, w in enumerate(POOL_WINDOWS):
        ext = ubuf_ref[:, g * POOL_CH:(g + 1) * POOL_CH]
        acc = ext
        span = 1
        while span < w:
            acc = acc + pltpu.roll(acc, span, axis=0)
            span *= 2
        mean = acc[POOL_HALO:, :] / jnp.minimum(t, float(w))
        deltas.append((mean - ext[POOL_HALO:, :]).astype(BF16))

    scale = LOG2_E / math.sqrt(HEAD_DIM)
    q_t = lax.dot_general(wqv_ref[0:ATTN_W, :], xb, NT_DIMS, preferred_element_type=F32)
    v_t = lax.dot_general(wqv_ref[ATTN_W:, :], xb, NT_DIMS, preferred_element_type=F32)
    hq = tm // 2
    zeros = jnp.zeros((HEAD_DIM, hq), BF16)
    ones_tile = jnp.where(
        lax.broadcasted_iota(jnp.int32, (BF16_SUBLANE_TILE, tk), 0) == 0, 1.0, 0.0).astype(BF16)
    for h in range(N_HEADS):
        f0 = h * HEAD_W
        q1 = (q_t[f0:f0 + HEAD_DIM] * scale).astype(BF16)
        q2 = (q_t[f0 + HEAD_DIM:f0 + HEAD_W] * scale).astype(BF16)
        for half in range(2):
            qs = slice(half * hq, (half + 1) * hq)
            c1 = slice(half * tm, half * tm + hq)
            c2 = slice(half * tm + hq, (half + 1) * tm)
            qz_ref[f0:f0 + HEAD_DIM, c1] = q1[:, qs]
            qz_ref[f0 + HEAD_DIM:f0 + HEAD_W, c1] = zeros
            qz_ref[f0:f0 + HEAD_DIM, c2] = zeros
            qz_ref[f0 + HEAD_DIM:f0 + HEAD_W, c2] = q2[:, qs]
        r0 = h * V_ROWS
        for c in range(tm // tk):
            vt_ref[c, r0:r0 + HEAD_W, :] = v_t[f0:f0 + HEAD_W, c * tk:(c + 1) * tk].astype(BF16)
            vt_ref[c, r0 + HEAD_W:r0 + V_ROWS, :] = ones_tile

    k_ref[...] = proj(0, ATTN_W).astype(BF16)
    gattn_ref[...] = proj(ATTN_W, ATTN_W)

    for g in range(N_POOL_GROUPS):
        cols = slice(g * POOL_CH, (g + 1) * POOL_CH)
        y = jnp.dot(deltas[g], wpool_ref[g].astype(BF16), preferred_element_type=F32)
        pool_ref[:, cols] = (y * pool_gate[:, cols]).astype(BF16)


def _in_proj_pool(x, w_in, w_pool, layer, pool_scale):
    B, S, _ = x.shape
    tm, tk = SEQ_TILE, KEY_TILE
    nt = S // tm
    kern = functools.partial(_in_proj_pool_kernel, tm=tm, tk=tk)
    return pl.pallas_call(
        kern,
        grid=(B, nt),
        in_specs=[
            pl.BlockSpec((None, tm, D_MODEL), lambda b, i: (b, i, 0)),
            pl.BlockSpec((None, D_MODEL, IN_W), lambda b, i: (layer, 0, 0),
                         pipeline_mode=pl.Buffered(1)),
            pl.BlockSpec((None, N_POOL_GROUPS, POOL_CH, POOL_CH), lambda b, i: (layer, 0, 0, 0)),
            pl.BlockSpec((1, POOL_W), lambda b, i: (0, 0)),
        ],
        out_specs=[
            pl.BlockSpec((None, None, ATTN_W, 2 * tm), lambda b, i: (b, i, 0, 0)),
            pl.BlockSpec((None, tm, ATTN_W), lambda b, i: (b, i, 0)),
            pl.BlockSpec((None, tm // tk, N_HEADS * V_ROWS, tk), lambda b, i: (b, i, 0, 0)),
            pl.BlockSpec((None, tm, ATTN_W), lambda b, i: (b, i, 0)),
            pl.BlockSpec((None, tm, POOL_W), lambda b, i: (b, i, 0)),
        ],
        out_shape=[
            jax.ShapeDtypeStruct((B, nt, ATTN_W, 2 * tm), BF16),
            jax.ShapeDtypeStruct((B, S, ATTN_W), BF16),
            jax.ShapeDtypeStruct((B, S // tk, N_HEADS * V_ROWS, tk), BF16),
            jax.ShapeDtypeStruct((B, S, ATTN_W), F32),
            jax.ShapeDtypeStruct((B, S, POOL_W), BF16),
        ],
        scratch_shapes=[
            pltpu.VMEM((POOL_HALO + tm, POOL_W), F32),
            pltpu.VMEM((2 * ATTN_W, D_MODEL), BF16),
            pltpu.VMEM((D_MODEL, REST_W), BF16),
        ],
        compiler_params=pltpu.CompilerParams(
            dimension_semantics=("arbitrary", "arbitrary"),
            vmem_limit_bytes=V7X_VMEM_LIMIT_BYTES),
        name="in_proj_pool",
    )(x, w_in, w_pool, pool_scale)


def _diff_attn_kernel(lq1_ref, lk1_ref, lq2_ref, lk2_ref, subln_ref,
                      qz_ref, k_ref, vt_ref, g_ref, o_ref,
                      m_ref, acc_ref, s_ref, bias_ref, *, qt, tq, tk, lam_init):
    qi = pl.program_id(1)
    first_diag = qi * qt
    hq = tq // 2

    @pl.when((pl.program_id(0) == 0) & (qi == 0))
    def _():
        key = lax.broadcasted_iota(jnp.int32, (tk, tq), 0)
        col = lax.broadcasted_iota(jnp.int32, (tk, tq), 1)
        qry = hq + jnp.where(col >= hq, col - hq, col)
        bias_ref[0] = jnp.zeros((tk, tq), F32)
        bias_ref[1] = jnp.where(key <= qry, 0.0, -jnp.inf).astype(F32)

    m_ref[...] = jnp.full(m_ref.shape, -jnp.inf, F32)
    acc_ref[...] = jnp.zeros(acc_ref.shape, F32)

    def scores(j, unit, bias_slot, slot, first_keys_only=False):
        sub, h, half = unit
        qcols = slice(half * tq, (half + 1) * tq)
        nk = hq if first_keys_only else tk
        k_start = pl.multiple_of(j * tk, tk)
        kj = k_ref[pl.ds(k_start, nk), h * HEAD_W:(h + 1) * HEAD_W]
        s_t = jnp.dot(kj, qz_ref[sub, h * HEAD_W:(h + 1) * HEAD_W, qcols],
                      preferred_element_type=F32)
        if first_keys_only:
            s_t = s_t + bias_ref[1, hq:tk, :]
        elif bias_slot is not None:
            s_t = s_t + bias_ref[bias_slot]
        m_prev = m_ref[sub, h, :, qcols]
        m_new = jnp.maximum(m_prev, jnp.max(s_t, axis=0, keepdims=True))
        m_ref[sub, h, :, qcols] = m_new
        s_ref[slot, 0:nk, :] = s_t
        return m_prev, m_new

    def accumulate(j, unit, slot, m_prev, m_new, first_keys_only=False):
        sub, h, half = unit
        qcols = slice(half * tq, (half + 1) * tq)
        nk = hq if first_keys_only else tk
        vj = vt_ref[j, h * V_ROWS:(h + 1) * V_ROWS, 0:nk]
        alpha = jnp.exp2(m_prev - m_new)
        p_t = jnp.exp2(s_ref[slot, 0:nk, :] - m_new).astype(BF16)
        acc_ref[sub, h, :, qcols] = (alpha * acc_ref[sub, h, :, qcols]
                                     + jnp.dot(vj, p_t, preferred_element_type=F32))

    head_units = [(h, half) for h in range(N_HEADS) for half in (1, 0)]
    full_units = [(sub, h, half) for sub in range(qt) for (h, half) in head_units]
    first_unit = full_units[0]
    assert len(full_units) % 2 == 0

    def first_unit_bias_slot(j):
        return (j == first_diag).astype(jnp.int32)

    def body(j, pending):
        for idx, unit in enumerate(full_units):
            if idx + 1 < len(full_units):
                nxt = scores(j, full_units[idx + 1], None, (idx + 1) % 2)
            else:
                nxt = scores(j + 1, first_unit, first_unit_bias_slot(j + 1), 0)
            accumulate(j, unit, idx % 2, *pending)
            pending = nxt
        return pending

    pending = scores(0, first_unit, first_unit_bias_slot(0), 0)
    pending = lax.fori_loop(0, first_diag, body, pending)

    diag_items = []
    for d in range(qt):
        for sub in range(d, qt):
            for (h, half) in head_units:
                on_diag = sub == d
                diag_items.append((d, (sub, h, half),
                                   1 if on_diag and half == 1 else None,
                                   on_diag and half == 0))
    assert len(diag_items) % 2 == 0
    for idx, (d, unit, bias_slot, first_only) in enumerate(diag_items):
        if idx + 1 < len(diag_items):
            d2, unit2, bias2, first_only2 = diag_items[idx + 1]
            nxt = scores(first_diag + d2, unit2, bias2, (idx + 1) % 2, first_only2)
        else:
            nxt = None
        accumulate(first_diag + d, unit, idx % 2, *pending, first_keys_only=first_only)
        pending = nxt

    lam = (jnp.exp(jnp.sum(lq1_ref[...] * lk1_ref[...], axis=-1, keepdims=True))
           - jnp.exp(jnp.sum(lq2_ref[...] * lk2_ref[...], axis=-1, keepdims=True))
           + lam_init)
    for sub in range(qt):
        for h in range(N_HEADS):
            cols = slice(h * HEAD_W, (h + 1) * HEAD_W)
            acc = acc_ref[sub, h]
            inv_l = 1.0 / acc[HEAD_W:HEAD_W + 1, :]
            for half in range(2):
                rows = slice(sub * tq + half * hq, sub * tq + (half + 1) * hq)
                c1 = slice(half * tq, half * tq + hq)
                c2 = slice(half * tq + hq, (half + 1) * tq)
                o_t = (acc[0:HEAD_W, c1] * inv_l[:, c1]
                       - acc[0:HEAD_W, c2] * (lam * inv_l[:, c2]))
                ms = jnp.mean(o_t * o_t, axis=0, keepdims=True)
                o_t = o_t * (lax.rsqrt(ms + LN_EPS) * (1.0 - lam_init))
                o = o_t.T * subln_ref[...]
                o_ref[rows, cols] = (o * _silu(g_ref[rows, cols])).astype(BF16)


def _diff_attn(qz, k, vt, g_attn, lam_q1, lam_k1, lam_q2, lam_k2, subln_g, *, lam_init):
    B, S, _ = k.shape
    tq, tk, qt = SEQ_TILE, KEY_TILE, Q_SUBTILES
    assert tq == tk, "a diagonal block must be square for the shared causal bias tile"
    nt = S // tq
    kern = functools.partial(_diff_attn_kernel, qt=qt, tq=tq, tk=tk, lam_init=lam_init)
    vec = lambda n: pl.BlockSpec((1, n), lambda b, i: (0, 0))
    return pl.pallas_call(
        kern,
        grid=(B, nt // qt),
        in_specs=[
            vec(HEAD_DIM), vec(HEAD_DIM), vec(HEAD_DIM), vec(HEAD_DIM), vec(HEAD_W),
            pl.BlockSpec((None, qt, ATTN_W, 2 * tq), lambda b, i: (b, i, 0, 0)),
            pl.BlockSpec((None, S, ATTN_W), lambda b, i: (b, 0, 0)),
            pl.BlockSpec((None, S // tk, N_HEADS * V_ROWS, tk), lambda b, i: (b, 0, 0, 0)),
            pl.BlockSpec((None, qt * tq, ATTN_W), lambda b, i: (b, i, 0)),
        ],
        out_specs=pl.BlockSpec((None, qt * tq, ATTN_W), lambda b, i: (b, i, 0)),
        out_shape=jax.ShapeDtypeStruct((B, S, ATTN_W), BF16),
        scratch_shapes=[
            pltpu.VMEM((qt, N_HEADS, 1, 2 * tq), F32),
            pltpu.VMEM((qt, N_HEADS, V_ROWS, 2 * tq), F32),
            pltpu.VMEM((2, tk, tq), F32),
            pltpu.VMEM((2, tk, tq), F32),
        ],
        compiler_params=pltpu.CompilerParams(
            dimension_semantics=("arbitrary", "arbitrary"),
            vmem_limit_bytes=V7X_VMEM_LIMIT_BYTES),
        name="diff_attn",
    )(lam_q1, lam_k1, lam_q2, lam_k2, subln_g, qz, k, vt, g_attn)


def _out_proj_kernel(x_ref, attn_ref, pool_ref, p_ref, wout32_ref, lng_ref, lnb_ref,
                     wpe32_ref, wpg32_ref, bpg_ref, o_ref, wout_ref, wpe_ref, wpg_ref,
                     *, alpha):
    @pl.when(pl.program_id(0) == 0)
    def _():
        wout_ref[...] = wout32_ref[...].astype(BF16)
        wpe_ref[...] = wpe32_ref[...].astype(BF16)
        wpg_ref[...] = wpg32_ref[...].astype(BF16)

    branches = jnp.concatenate([attn_ref[...], pool_ref[...]], axis=1)
    mix = jnp.dot(branches, wout_ref[...], preferred_element_type=F32)
    pe = jnp.dot(p_ref[...].astype(BF16), wpe_ref[...], preferred_element_type=F32)
    r = alpha * x_ref[...] + mix
    mu = jnp.mean(r, axis=-1, keepdims=True)
    cen = r - mu
    var = jnp.mean(cen * cen, axis=-1, keepdims=True)
    h = cen * lax.rsqrt(var + LN_EPS) * lng_ref[...] + lnb_ref[...]
    logits = jnp.dot(h.astype(BF16), wpg_ref[...], preferred_element_type=F32) + bpg_ref[...]
    gate = 1.0 / (1.0 + jnp.exp(-logits))
    o_ref[...] = h + pe * gate


def _out_proj(x2, attn2, pool2, p2, w_out, ln_g, ln_b, w_pe, w_pg, b_pg, layer, *, alpha):
    R = x2.shape[0]
    tm = OUT_TILE
    kern = functools.partial(_out_proj_kernel, alpha=alpha)
    rows = lambda n: pl.BlockSpec((tm, n), lambda i: (i, 0))
    full = lambda a, b: pl.BlockSpec((a, b), lambda i: (0, 0))
    once = lambda a, b: pl.BlockSpec((None, a, b), lambda i: (layer, 0, 0),
                                     pipeline_mode=pl.Buffered(1))
    return pl.pallas_call(
        kern,
        grid=(R // tm,),
        in_specs=[
            rows(D_MODEL), rows(ATTN_W), rows(POOL_W), rows(PLE_DIM),
            once(ATTN_W + POOL_W, D_MODEL), full(1, D_MODEL), full(1, D_MODEL),
            once(PLE_DIM, D_MODEL), once(D_MODEL, D_MODEL), full(1, D_MODEL),
        ],
        out_specs=rows(D_MODEL),
        out_shape=jax.ShapeDtypeStruct((R, D_MODEL), F32),
        scratch_shapes=[
            pltpu.VMEM((ATTN_W + POOL_W, D_MODEL), BF16),
            pltpu.VMEM((PLE_DIM, D_MODEL), BF16),
            pltpu.VMEM((D_MODEL, D_MODEL), BF16),
        ],
        compiler_params=pltpu.CompilerParams(
            dimension_semantics=("arbitrary",),
            vmem_limit_bytes=V7X_VMEM_LIMIT_BYTES),
        name="out_proj_ln_ple",
    )(x2, attn2, pool2, p2, w_out, ln_g, ln_b, w_pe, w_pg, b_pg)


def kernel(x, p, w_in, lam_q1, lam_k1, lam_q2, lam_k2, subln_g, w_pool, pool_scale,
           w_out, ln_g, ln_b, w_pe, w_pg, b_pg):
    B, S, D = x.shape
    depth = w_in.shape[0]
    h = x
    for i in range(depth):
        alpha = (2.0 * depth) ** 0.25
        qz, k, vt, g_attn, pool_mix = _in_proj_pool(
            h, w_in, w_pool, i, pool_scale[i][None, :])
        attn = _diff_attn(
            qz, k, vt, g_attn, lam_q1[i][None, :], lam_k1[i][None, :], lam_q2[i][None, :],
            lam_k2[i][None, :], subln_g[i][None, :], lam_init=_lambda_init(i))
        out = _out_proj(
            h.reshape(B * S, D), attn.reshape(B * S, ATTN_W), pool_mix.reshape(B * S, POOL_W),
            p[i].reshape(B * S, PLE_DIM), w_out, ln_g[i][None, :],
            ln_b[i][None, :], w_pe, w_pg, b_pg[i][None, :], i, alpha=alpha)
        h = out.reshape(B, S, D)
    return h
```
